```python
import jax, jax.numpy as jnp
from jax import lax
import numpy as np

D_MODEL = 1024
BATCH = 32
SEQ = 256
DEPTH = 4
DEC_BATCH = 4
DEC_SEQ = 4096
PAST_LEN = 256

GRID_W = 64
MIX_W = D_MODEL
BRANCH_W = MIX_W // 2
H_A = 4
DK_A = BRANCH_W // (2 * H_A)
DV_A = BRANCH_W // H_A
GATE_RANK = 16
GATE_NORMALIZER = 16.0
CHUNK_A = 64
H_B = 4
DH_B = BRANCH_W // H_B
CHUNK_B = 128
POOL_WINDOWS = (2, 4, 8, 16)
G_C = 4
DG_C = BRANCH_W // G_C
CONV_W = 3
N_EVEN = (DEPTH + 1) // 2
N_ODD = DEPTH // 2
EVEN_SPLITS = (H_A * DK_A, H_A * DK_A, BRANCH_W, BRANCH_W, GATE_RANK, GATE_RANK,
               BRANCH_W, BRANCH_W, BRANCH_W)
EVEN_IN = sum(EVEN_SPLITS)
ODD_SPLITS = (BRANCH_W,) * 6
ODD_IN = sum(ODD_SPLITS)
EPS = 1e-6

kernel_name = 'hybrid_gla_sgu_pool_conv_diffusion_step'


def split_cols(z, sizes):
    idx = np.cumsum(sizes)[:-1].tolist()
    return jnp.split(z, idx, axis=-1)


def rmsnorm(x, g):
    xf = x.astype(jnp.float32)
    y = xf * lax.rsqrt(jnp.mean(xf * xf, axis=-1, keepdims=True) + EPS)
    return (y * g.astype(jnp.float32)).astype(x.dtype)


def adaln(cond, w, b):
    m = jax.nn.silu(cond) @ w + b
    shift, scale, gate = jnp.split(m, 3, axis=-1)
    return shift[:, None], scale[:, None], gate[:, None]


def to_col_major(x, rows):
    B, T, C = x.shape
    return x.reshape(B, rows, GRID_W, C).transpose(0, 2, 1, 3).reshape(B, T, C)


def from_col_major(x, rows):
    B, T, C = x.shape
    return x.reshape(B, GRID_W, rows, C).transpose(0, 2, 1, 3).reshape(B, T, C)


def gla_scan(q, k, v, g, s0):
    dt = v.dtype
    B, T, H, DK = q.shape
    DV = v.shape[-1]
    n = T // CHUNK_A

    def chunks(a):
        return a.astype(jnp.float32).reshape(B, n, CHUNK_A, H, a.shape[-1]).swapaxes(0, 1)

    tri = jnp.tril(jnp.ones((CHUNK_A, CHUNK_A), bool))[None, :, :, None, None]

    def step(S, inp):
        qc, kc, vc, gc = inp
        b = jnp.cumsum(gc, axis=1)
        o_inter = jnp.einsum('bihk,bhkv->bihv', qc * jnp.exp(b), S)
        diff = b[:, :, None] - b[:, None]
        decay = jnp.where(tri, jnp.exp(jnp.where(tri, diff, 0.0)), 0.0)
        att = jnp.einsum('bihk,bjhk,bijhk->bhij', qc, kc, decay)
        o_intra = jnp.einsum('bhij,bjhv->bihv', att, vc)
        b_last = b[:, -1]
        S = jnp.exp(b_last)[..., None] * S + jnp.einsum(
            'bjhk,bjhv->bhkv', kc * jnp.exp(b_last[:, None] - b), vc)
        return S, o_inter + o_intra

    S, o = lax.scan(step, s0.astype(jnp.float32), (chunks(q), chunks(k), chunks(v), chunks(g)))
    o = o.swapaxes(0, 1).reshape(B, T, H, DV)
    return o.astype(dt), S.astype(dt)


def centred_mean(x, w):
    B, T, C = x.shape
    cs = jnp.concatenate([jnp.zeros((B, 1, C), jnp.float32),
                          jnp.cumsum(x.astype(jnp.float32), axis=1)], axis=1)
    t = jnp.arange(T)
    lo = jnp.clip(t - w // 2, 0, T)
    hi = jnp.clip(t + w - w // 2, 0, T)
    s = jnp.take(cs, hi, axis=1) - jnp.take(cs, lo, axis=1)
    cnt = (hi - lo).astype(jnp.float32)
    return (s / cnt[None, :, None]).astype(x.dtype)


def setup_inputs(seed: int = 0) -> dict:
    key = jax.random.key(seed)
    ks = jax.random.split(key, 24)
    f32 = jnp.float32
    nrm = lambda k, shape, s: jax.random.normal(k, shape, f32) * s
    return {
        'x_prompt': nrm(ks[0], (BATCH, SEQ, D_MODEL), 1.0),
        'x_sample': nrm(ks[1], (DEC_BATCH, DEC_SEQ, D_MODEL), 1.0),
        'c': nrm(ks[2], (DEC_BATCH, D_MODEL), 1.0),
        'state_gla': nrm(ks[3], (DEC_BATCH, N_EVEN, 2, H_A, DK_A, DV_A), 1.0),
        'c_ctx': nrm(ks[4], (D_MODEL,), 1.0),
        'w_ada': nrm(ks[5], (DEPTH, D_MODEL, 3 * D_MODEL), 0.2 * D_MODEL ** -0.5),
        'b_ada': nrm(ks[6], (DEPTH, 3 * D_MODEL), 0.02),
        'norm_g': 1.0 + nrm(ks[7], (DEPTH, D_MODEL), 0.02),
        'w_in_even': nrm(ks[8], (N_EVEN, D_MODEL, EVEN_IN), D_MODEL ** -0.5),
        'w_in_odd': nrm(ks[9], (N_ODD, D_MODEL, ODD_IN), D_MODEL ** -0.5),
        'w_out': nrm(ks[10], (DEPTH, MIX_W, D_MODEL), 0.5 * MIX_W ** -0.5),
        'w_gk': nrm(ks[11], (N_EVEN, 2, GATE_RANK, H_A * DK_A), GATE_RANK ** -0.5),
        'b_gk': nrm(ks[12], (N_EVEN, 2, H_A * DK_A), 0.1),
        'gla_norm_g': 1.0 + nrm(ks[13], (N_EVEN, DV_A), 0.02),
        'sgu_norm_g': 1.0 + nrm(ks[14], (N_EVEN, BRANCH_W), 0.02),
        'w_s': nrm(ks[15], (N_EVEN, H_B, CHUNK_B, CHUNK_B), CHUNK_B ** -0.5),
        'b_s': 1.0 + nrm(ks[16], (N_EVEN, H_B, CHUNK_B), 0.1),
        'w_pool': nrm(ks[17], (N_ODD, G_C, DG_C, DG_C), DG_C ** -0.5),
        'pool_scale': 1.0 + nrm(ks[18], (N_ODD, BRANCH_W), 0.1),
        'w_conv': nrm(ks[19], (N_ODD, CONV_W, BRANCH_W), CONV_W ** -0.5),
        'final_norm_g': 1.0 + nrm(ks[20], (D_MODEL,), 0.02),
    }


def reference(x_prompt, x_sample, c, state_gla, c_ctx, w_ada, b_ada, norm_g, w_in_even,
              w_in_odd, w_out, w_gk, b_gk, gla_norm_g, sgu_norm_g, w_s, b_s, w_pool,
              pool_scale, w_conv, final_norm_g):

    def mix_even(z, j, s0_f, s0_b):
        B, T, _ = z.shape
        q, k, v, ga, lr_f, lr_b, u, vs, gb = split_cols(z, EVEN_SPLITS)
        q = q.reshape(B, T, H_A, DK_A) * (DK_A ** -0.5)
        k = k.reshape(B, T, H_A, DK_A)
        v = v.reshape(B, T, H_A, DV_A)

        def log_decay(lr, d):
            gl = (lr @ w_gk[j, d] + b_gk[j, d]).astype(jnp.float32)
            return (jax.nn.log_sigmoid(gl) / GATE_NORMALIZER).reshape(B, T, H_A, DK_A)

        flip = lambda a: a[:, ::-1]
        o_f, s_f = gla_scan(q, k, v, log_decay(lr_f, 0), s0_f)
        o_r, s_b = gla_scan(flip(q), flip(k), flip(v), flip(log_decay(lr_b, 1)), s0_b)
        o = o_f + flip(o_r)
        o_a = rmsnorm(o, gla_norm_g[j]).reshape(B, T, BRANCH_W) * jax.nn.silu(ga)
        n = T // CHUNK_B
        vs = rmsnorm(vs, sgu_norm_g[j]).reshape(B, n, CHUNK_B, H_B, DH_B)
        sp = jnp.einsum('hij,bnjhd->bnihd', w_s[j], vs) + b_s[j].T[None, None, :, :, None]
        o_b = u * sp.reshape(B, T, BRANCH_W) * jax.nn.silu(gb)
        return jnp.concatenate([o_a, o_b], axis=-1), s_f, s_b

    def mix_odd(z, j):
        B, T, _ = z.shape
        xc, gc, xd, bd, cd, gd = split_cols(z, ODD_SPLITS)
        pooled = jnp.concatenate(
            [centred_mean(xc[..., i * DG_C:(i + 1) * DG_C], w) for i, w in enumerate(POOL_WINDOWS)],
            axis=-1) - xc
        pooled = jnp.einsum('btgc,gcd->btgd', pooled.reshape(B, T, G_C, DG_C), w_pool[j])
        o_c = pooled.reshape(B, T, BRANCH_W) * pool_scale[j] * jax.nn.silu(gc)
        u = cd * xd
        up = jnp.pad(u, ((0, 0), (1, 1), (0, 0)))
        y = up[:, :-2] * w_conv[j, 0] + up[:, 1:-1] * w_conv[j, 1] + up[:, 2:] * w_conv[j, 2]
        o_d = bd * y * jax.nn.silu(gd)
        return jnp.concatenate([o_c, o_d], axis=-1)

    def run_stream(x, cond, gla_init, rows, latent):
        finals = []
        for l in range(DEPTH):
            shift, scale, gate = adaln(cond, w_ada[l], b_ada[l])
            h = rmsnorm(x, norm_g[l]) * (1.0 + scale) + shift
            col = latent and (l // 2) % 2 == 1
            if col:
                h = to_col_major(h, rows)
            j = l // 2
            if l % 2 == 0:
                out, s_f, s_b = mix_even(h @ w_in_even[j], j, gla_init[j][0], gla_init[j][1])
                finals.append(jnp.stack([s_f, s_b], axis=1))
            else:
                out = mix_odd(h @ w_in_odd[j], j)
            out = out @ w_out[l]
            if col:
                out = from_col_major(out, rows)
            x = x + gate * out
        return rmsnorm(x, final_norm_g), finals

    Bp = x_prompt.shape[0]
    zero_s = jnp.zeros((Bp, H_A, DK_A, DV_A), x_prompt.dtype)
    ctx_init = [(zero_s, zero_s) for _ in range(N_EVEN)]
    y_prompt, finals = run_stream(x_prompt, c_ctx[None], ctx_init, 0, False)
    new_state_gla = jnp.stack(finals, axis=1)

    rows = x_sample.shape[1] // GRID_W
    lat_init = [(state_gla[:, j, 0], state_gla[:, j, 1]) for j in range(N_EVEN)]
    y_sample, _ = run_stream(x_sample, c, lat_init, rows, True)

    return (y_prompt, y_sample, new_state_gla)
```

```python
import functools

import numpy as np
import jax
import jax.numpy as jnp
from jax import lax
from jax.experimental import pallas as pl
from jax.experimental.pallas import tpu as pltpu

F32 = jnp.float32
BF16 = jnp.bfloat16

D_MODEL = 1024
DEPTH = 4
GRID_W = 64
BRANCH_W = 512
H_A = 4
DK_A = 64
DV_A = 128
GATE_RANK = 16
GATE_NORMALIZER = 16.0
CHUNK_A = 64
H_B = 4
DH_B = 128
CHUNK_B = 128
POOL_WINDOWS = (2, 4, 8, 16)
G_C = 4
DG_C = 128
EPS = 1e-6

LANES = 128
TOKEN_BLOCK = 256
HALO = 8
MOD_ROWS = 8
VMEM_LIMIT = 56 * 1024 * 1024


def _dot(a, b):
    return jnp.dot(a, b, preferred_element_type=F32)


def _dot_nt(a, b):
    return lax.dot_general(a, b, (((1,), (1,)), ((), ())), preferred_element_type=F32)


def _dot_tn(a, b):
    return lax.dot_general(a, b, (((0,), (0,)), ((), ())), preferred_element_type=F32)


def _rms(x, g):
    ms = jnp.mean(x * x, axis=-1, keepdims=True)
    return x * lax.rsqrt(ms + EPS) * g


def _silu(x):
    return x * (1.0 / (1.0 + jnp.exp(-x)))


def _log_sigmoid(x):
    return jnp.minimum(x, 0.0) - jnp.log1p(jnp.exp(-jnp.abs(x)))


def _split3_bf16(x):
    x1 = x.astype(BF16)
    r1 = x - x1.astype(F32)
    x2 = r1.astype(BF16)
    r2 = r1 - x2.astype(F32)
    x3 = r2.astype(BF16)
    return jnp.concatenate([x1, x2, x3], axis=1)


def _cumsum_blocks(tri, g):
    w = g.shape[1]
    bc = _dot(tri, _split3_bf16(g))
    return bc[:, 0:w] + bc[:, w:2 * w] + bc[:, 2 * w:3 * w]


def _gla_chunks(q, k, v, b, st_scr, reverse):
    tb = q.shape[0]
    n_chunks = tb // CHUNK_A
    row = lax.broadcasted_iota(jnp.int32, (CHUNK_A, CHUNK_A), 0)
    col = lax.broadcasted_iota(jnp.int32, (CHUNK_A, CHUNK_A), 1)
    causal = (col >= row) if reverse else (col <= row)
    lane = lax.broadcasted_iota(jnp.int32, (CHUNK_A, LANES), 1)
    head_masks = [lane < DK_A, lane >= DK_A]
    outs = [None] * n_chunks
    order = range(n_chunks - 1, -1, -1) if reverse else range(n_chunks)
    for c in order:
        r0 = c * CHUNK_A
        bc = b[r0:r0 + CHUNK_A]
        b_end = bc[0:1] if reverse else bc[CHUNK_A - 1:CHUNK_A]
        qs = q[r0:r0 + CHUNK_A] * jnp.exp(bc)
        kc = k[r0:r0 + CHUNK_A]
        kt = (kc * jnp.exp(-bc)).astype(BF16)
        kh = kc * jnp.exp(b_end - bc)
        dec = jnp.exp(b_end)
        heads = []
        for h in range(H_A):
            p = h // 2
            m = head_masks[h % 2]
            sl = slice(p * LANES, (p + 1) * LANES)
            qm = jnp.where(m, qs[:, sl], 0.0).astype(BF16)
            att = _dot_nt(qm, kt[:, sl])
            att = jnp.where(causal, att, 0.0).astype(BF16)
            vh = v[r0:r0 + CHUNK_A, h * DV_A:(h + 1) * DV_A]
            st = st_scr[h]
            o = _dot(att, vh) + _dot_nt(qm, st.astype(BF16))
            heads.append(o)
            khm = jnp.where(m, kh[:, sl], 0.0).astype(BF16)
            st_scr[h] = st * dec[:, sl] + _dot_tn(vh, khm)
        outs[c] = heads
    return outs


def _write_final_state(st_scr, stout_ref):
    for h in range(H_A):
        st_t = st_scr[h].T
        off = (h % 2) * DK_A
        stout_ref[h] = st_t[off:off + DK_A, :]


def _even_fwd_kernel(x_ref, mod_ref, ng_ref, win_ref, wlr_ref, wgk_ref, bgk_ref, sng_ref, ws_ref,
                     bs_ref, tri_ref, st0_ref,
                     qk_ref, v_ref, lr_ref, of_ref, ga_ref, sgu_ref, stout_ref, st_scr):
    t = pl.program_id(1)
    n_t = pl.num_programs(1)

    @pl.when(t == 0)
    def _():
        st_scr[...] = st0_ref[...]

    x = x_ref[...]
    shift = mod_ref[:, 0:D_MODEL]
    scale = mod_ref[:, D_MODEL:2 * D_MODEL]
    h = (_rms(x, ng_ref[...]) * (1.0 + scale) + shift).astype(BF16)
    z = _dot(h, win_ref[...])
    lr = _dot(h, wlr_ref[...])
    q = z[:, 0:256] * (DK_A ** -0.5)
    k = z[:, 256:512]
    v = z[:, 512:1024].astype(BF16)
    u = z[:, 1536:2048]
    vs = z[:, 2048:2560]
    gb = z[:, 2560:3072]

    qk_ref[:, 0:256] = q
    qk_ref[:, 256:512] = k
    v_ref[...] = v
    lr_ref[...] = lr
    ga_ref[...] = z[:, 1024:1536]

    gl = _dot(lr.astype(BF16), wgk_ref[...]) + bgk_ref[...]
    g = _log_sigmoid(gl) * (1.0 / GATE_NORMALIZER)
    b = _cumsum_blocks(tri_ref[...], g)
    outs = _gla_chunks(q, k, v, b, st_scr, reverse=False)
    for c, heads in enumerate(outs):
        for hd, o in enumerate(heads):
            of_ref[c * CHUNK_A:(c + 1) * CHUNK_A, hd * DV_A:(hd + 1) * DV_A] = o

    @pl.when(t == n_t - 1)
    def _():
        _write_final_state(st_scr, stout_ref)

    vsn = _rms(vs, sng_ref[...]).astype(BF16)
    tb = x.shape[0]
    for n in range(tb // CHUNK_B):
        rs = slice(n * CHUNK_B, (n + 1) * CHUNK_B)
        for hb in range(H_B):
            cs = slice(hb * DH_B, (hb + 1) * DH_B)
            sp = _dot(ws_ref[hb], vsn[rs, cs]) + bs_ref[hb]
            sgu_ref[rs, cs] = (u[rs, cs] * sp * _silu(gb[rs, cs])).astype(BF16)


def _even_bwd_kernel(x_ref, mod_ref, qk_ref, v_ref, lr_ref, of_ref, ga_ref, sgu_ref, wgk_ref, bgk_ref,
                     gng_ref, wout_ref, tri_ref, st0_ref,
                     xo_ref, stout_ref, st_scr, oa_scr):
    t = pl.program_id(1)
    n_t = pl.num_programs(1)

    @pl.when(t == 0)
    def _():
        st_scr[...] = st0_ref[...]

    q = qk_ref[:, 0:256]
    k = qk_ref[:, 256:512]
    v = v_ref[...]
    gl = _dot(lr_ref[...].astype(BF16), wgk_ref[...]) + bgk_ref[...]
    g = _log_sigmoid(gl) * (1.0 / GATE_NORMALIZER)
    b = _cumsum_blocks(tri_ref[...], g)
    outs = _gla_chunks(q, k, v, b, st_scr, reverse=True)

    @pl.when(t == n_t - 1)
    def _():
        _write_final_state(st_scr, stout_ref)

    gng = gng_ref[...]
    for c, heads in enumerate(outs):
        rs = slice(c * CHUNK_A, (c + 1) * CHUNK_A)
        for hd, o_b in enumerate(heads):
            cs = slice(hd * DV_A, (hd + 1) * DV_A)
            o = of_ref[rs, cs] + o_b
            oa_scr[rs, cs] = (_rms(o, gng) * _silu(ga_ref[rs, cs])).astype(BF16)

    out = _dot(oa_scr[...], wout_ref[0:BRANCH_W, :]) + _dot(sgu_ref[...], wout_ref[BRANCH_W:2 * BRANCH_W, :])
    gate = mod_ref[:, 2 * D_MODEL:3 * D_MODEL]
    xo_ref[...] = x_ref[...] + gate * out


def _odd_kernel(xp_ref, x_ref, xn_ref, mod_ref, ng_ref, win_ref, wpool_ref, pscale_ref, wconv_ref,
                wout_ref, fng_ref, xo_ref, xc_scr, u_scr, mix_scr, *, seq_len, colmajor_out, final_norm):
    t = pl.program_id(1)
    n_t = pl.num_programs(1)
    tb = x_ref.shape[0]

    x = x_ref[...]
    xe = jnp.concatenate([xp_ref[...], x, xn_ref[...]], axis=0)
    shift = mod_ref[:, 0:D_MODEL]
    scale = mod_ref[:, D_MODEL:2 * D_MODEL]
    h = (_rms(xe, ng_ref[...]) * (1.0 + scale) + shift).astype(BF16)
    z = _dot(h, win_ref[...])

    rowe = lax.broadcasted_iota(jnp.int32, (tb + 2 * HALO, 1), 0)
    valid = jnp.logical_and(jnp.logical_or(rowe >= HALO, t > 0),
                            jnp.logical_or(rowe < tb + HALO, t < n_t - 1))
    xc_scr[...] = jnp.where(valid, z[:, 0:512], 0.0)
    u_scr[...] = jnp.where(valid, z[:, 2048:2560] * z[:, 1024:1536], 0.0)

    pos = t * tb + lax.broadcasted_iota(jnp.int32, (tb, 1), 0)
    mid = slice(HALO, HALO + tb)
    for i, w in enumerate(POOL_WINDOWS):
        cs = slice(i * DG_C, (i + 1) * DG_C)
        s = xc_scr[pl.ds(HALO - w // 2, tb), cs]
        for off in range(-(w // 2) + 1, w - w // 2):
            s = s + xc_scr[pl.ds(HALO + off, tb), cs]
        lo = jnp.clip(pos - w // 2, 0, seq_len)
        hi = jnp.clip(pos + (w - w // 2), 0, seq_len)
        cnt = (hi - lo).astype(F32)
        pooled = s / cnt - xc_scr[mid, cs]
        pc = _dot(pooled.astype(BF16), wpool_ref[i])
        mix_scr[:, cs] = (pc * pscale_ref[:, cs] * _silu(z[mid, 512 + i * DG_C:512 + (i + 1) * DG_C])).astype(BF16)

    y = (u_scr[pl.ds(HALO - 1, tb), :] * wconv_ref[0:1, :] + u_scr[mid, :] * wconv_ref[1:2, :]
         + u_scr[pl.ds(HALO + 1, tb), :] * wconv_ref[2:3, :])
    mix_scr[:, BRANCH_W:2 * BRANCH_W] = (z[mid, 1536:2048] * y * _silu(z[mid, 2560:3072])).astype(BF16)

    out = _dot(mix_scr[...], wout_ref[...])
    gate = mod_ref[:, 2 * D_MODEL:3 * D_MODEL]
    xn = x + gate * out
    if final_norm:
        xn = _rms(xn, fng_ref[...])
    if colmajor_out:
        for j in range(tb // GRID_W):
            xo_ref[:, j * D_MODEL:(j + 1) * D_MODEL] = xn[j * GRID_W:(j + 1) * GRID_W]
    else:
        xo_ref[...] = xn


def _ada_kernel(c_ref, w_ref, b_ref, o_ref):
    c = c_ref[...]
    o_ref[...] = jnp.dot(_silu(c), w_ref[...], preferred_element_type=F32,
                         precision=lax.Precision.HIGHEST) + b_ref[...]


def _params(n_axes=2):
    return pltpu.CompilerParams(dimension_semantics=("arbitrary",) * n_axes,
                                vmem_limit_bytes=VMEM_LIMIT)


def _ada_mod(conds, w_ada, b_ada):
    n_col = 4
    cw = 3 * D_MODEL // n_col
    return pl.pallas_call(
        _ada_kernel,
        grid=(DEPTH, n_col),
        in_specs=[pl.BlockSpec((MOD_ROWS, D_MODEL), lambda l, j: (0, 0)),
                  pl.BlockSpec((None, D_MODEL, cw), lambda l, j: (l, 0, j)),
                  pl.BlockSpec((None, 1, cw), lambda l, j: (l, 0, j))],
        out_specs=pl.BlockSpec((None, MOD_ROWS, cw), lambda l, j: (l, 0, j)),
        out_shape=jax.ShapeDtypeStruct((DEPTH, MOD_ROWS, 3 * D_MODEL), F32),
        compiler_params=_params(2),
        name="ada_mod",
    )(conds, w_ada, b_ada.reshape(DEPTH, 1, 3 * D_MODEL))


def _tri_blocks(tb, upper):
    r = np.arange(tb)
    same = (r[:, None] // CHUNK_A) == (r[None, :] // CHUNK_A)
    tri = (r[None, :] >= r[:, None]) if upper else (r[None, :] <= r[:, None])
    return jnp.asarray((same & tri).astype(np.float32), dtype=BF16)


def _const_spec(shape):
    nd = len(shape)
    return pl.BlockSpec(shape, lambda b, t: (0,) * nd)


def _even_layer(x, mod, mod_row, batch, seq, lw, st0_f, st0_b):
    tb = TOKEN_BLOCK
    n_t = seq // tb
    n_tok = batch * seq
    tok = lambda width: pl.BlockSpec((tb, width), lambda b, t: (b * n_t + t, 0))
    tok_rev = lambda width: pl.BlockSpec((tb, width), lambda b, t: (b * n_t + (n_t - 1 - t), 0))
    mod_spec = pl.BlockSpec((None, 1, 3 * D_MODEL), lambda b, t: (mod_row(b), 0, 0))
    st_in = pl.BlockSpec((None, H_A, LANES, LANES), lambda b, t: (b, 0, 0, 0))
    st_out = pl.BlockSpec((None, H_A, DK_A, DV_A), lambda b, t: (b, 0, 0, 0))
    st_shape = jax.ShapeDtypeStruct((batch, H_A, DK_A, DV_A), F32)

    qk, v, lr, of, ga, sgu, s_f = pl.pallas_call(
        _even_fwd_kernel,
        grid=(batch, n_t),
        in_specs=[tok(D_MODEL), mod_spec, _const_spec((1, D_MODEL)),
                  _const_spec((D_MODEL, 3072)), _const_spec((D_MODEL, LANES)),
                  _const_spec((LANES, 256)), _const_spec((1, 256)), _const_spec((1, BRANCH_W)),
                  _const_spec((H_B, CHUNK_B, CHUNK_B)), _const_spec((H_B, CHUNK_B, LANES)),
                  _const_spec((tb, tb)), st_in],
        out_specs=[tok(512), tok(512), tok(LANES), tok(512), tok(512), tok(512), st_out],
        out_shape=[jax.ShapeDtypeStruct((n_tok, 512), F32), jax.ShapeDtypeStruct((n_tok, 512), BF16),
                   jax.ShapeDtypeStruct((n_tok, LANES), F32), jax.ShapeDtypeStruct((n_tok, 512), F32),
                   jax.ShapeDtypeStruct((n_tok, 512), F32), jax.ShapeDtypeStruct((n_tok, 512), BF16),
                   st_shape],
        scratch_shapes=[pltpu.VMEM((H_A, LANES, LANES), F32)],
        compiler_params=_params(2),
        name="even_fwd",
    )(x, mod, lw["norm_g"], lw["w_in"], lw["w_lr"], lw["w_gk_f"], lw["b_gk_f"], lw["sgu_norm_g"],
      lw["w_s"], lw["b_s"], lw["tri_lo"], st0_f)

    x_new, s_b = pl.pallas_call(
        _even_bwd_kernel,
        grid=(batch, n_t),
        in_specs=[tok_rev(D_MODEL), mod_spec, tok_rev(512), tok_rev(512), tok_rev(LANES), tok_rev(512),
                  tok_rev(512), tok_rev(512), _const_spec((LANES, 256)), _const_spec((1, 256)),
                  _const_spec((1, DV_A)), _const_spec((D_MODEL, D_MODEL)), _const_spec((tb, tb)), st_in],
        out_specs=[tok_rev(D_MODEL), st_out],
        out_shape=[jax.ShapeDtypeStruct((n_tok, D_MODEL), F32), st_shape],
        scratch_shapes=[pltpu.VMEM((H_A, LANES, LANES), F32), pltpu.VMEM((tb, BRANCH_W), BF16)],
        compiler_params=_params(2),
        name="even_bwd",
    )(x, mod, qk, v, lr, of, ga, sgu, lw["w_gk_b"], lw["b_gk_b"], lw["gla_norm_g"], lw["w_out"],
      lw["tri_up"], st0_b)
    return x_new, s_f, s_b


def _odd_layer(x, mod, mod_row, batch, seq, lw, colmajor_out, final_norm, final_norm_g):
    tb = TOKEN_BLOCK
    n_t = seq // tb
    n_tok = batch * seq
    hb = tb // HALO
    n_halo_blocks = n_tok // HALO
    tok = pl.BlockSpec((tb, D_MODEL), lambda b, t: (b * n_t + t, 0))
    prev = pl.BlockSpec((HALO, D_MODEL), lambda b, t: (jnp.maximum((b * n_t + t) * hb - 1, 0), 0))
    nxt = pl.BlockSpec((HALO, D_MODEL),
                       lambda b, t: (jnp.minimum((b * n_t + t + 1) * hb, n_halo_blocks - 1), 0))
    mod_spec = pl.BlockSpec((None, 1, 3 * D_MODEL), lambda b, t: (mod_row(b), 0, 0))
    if colmajor_out:
        rows_per_block = tb // GRID_W
        out_spec = pl.BlockSpec((None, GRID_W, rows_per_block * D_MODEL), lambda b, t: (b, 0, t))
        out_shape = jax.ShapeDtypeStruct((batch, GRID_W, (seq // GRID_W) * D_MODEL), F32)
    else:
        out_spec = tok
        out_shape = jax.ShapeDtypeStruct((n_tok, D_MODEL), F32)
    kern = functools.partial(_odd_kernel, seq_len=seq, colmajor_out=colmajor_out, final_norm=final_norm)
    out = pl.pallas_call(
        kern,
        grid=(batch, n_t),
        in_specs=[prev, tok, nxt, mod_spec, _const_spec((1, D_MODEL)), _const_spec((D_MODEL, 3072)),
                  _const_spec((G_C, DG_C, DG_C)), _const_spec((1, BRANCH_W)), _const_spec((3, BRANCH_W)),
                  _const_spec((D_MODEL, D_MODEL)), _const_spec((1, D_MODEL))],
        out_specs=out_spec,
        out_shape=out_shape,
        scratch_shapes=[pltpu.VMEM((tb + 2 * HALO, BRANCH_W), F32), pltpu.VMEM((tb + 2 * HALO, BRANCH_W), F32),
                        pltpu.VMEM((tb, D_MODEL), BF16)],
        compiler_params=_params(2),
        name="odd_layer",
    )(x, x, x, mod, lw["norm_g"], lw["w_in"], lw["w_pool"], lw["pool_scale"], lw["w_conv"], lw["w_out"],
      final_norm_g)
    return out.reshape(n_tok, D_MODEL)


def _state_to_kernel_layout(s):
    st = jnp.swapaxes(s, -1, -2)
    parts = []
    for h in range(H_A):
        pad = ((0, 0), (0, 0), (DK_A, 0)) if h % 2 else ((0, 0), (0, 0), (0, DK_A))
        parts.append(jnp.pad(st[:, h], pad))
    return jnp.stack(parts, axis=1)


def kernel(x_prompt, x_sample, c, state_gla, c_ctx, w_ada, b_ada, norm_g, w_in_even, w_in_odd, w_out, w_gk,
           b_gk, gla_norm_g, sgu_norm_g, w_s, b_s, w_pool, pool_scale, w_conv, final_norm_g):
    bp, tp, _ = x_prompt.shape
    bs, ts, _ = x_sample.shape
    assert tp % TOKEN_BLOCK == 0 and ts % TOKEN_BLOCK == 0 and ts == GRID_W * GRID_W
    assert 1 + bs <= MOD_ROWS

    conds = jnp.concatenate([c_ctx[None], c, jnp.zeros((MOD_ROWS - 1 - bs, D_MODEL), F32)], axis=0)
    mod = _ada_mod(conds, w_ada, b_ada).reshape(DEPTH * MOD_ROWS, 1, 3 * D_MODEL)

    tri_lo = _tri_blocks(TOKEN_BLOCK, upper=False)
    tri_up = _tri_blocks(TOKEN_BLOCK, upper=True)
    fng = final_norm_g.reshape(1, D_MODEL)

    def even_weights(l):
        j = l // 2
        w = w_in_even[j]
        w_main = jnp.concatenate([w[:, 0:1536], w[:, 1568:3104]], axis=1).astype(BF16)
        w_lr = jnp.pad(w[:, 1536:1568], ((0, 0), (0, LANES - 2 * GATE_RANK))).astype(BF16)
        gk_f = jnp.pad(w_gk[j, 0], ((0, LANES - GATE_RANK), (0, 0))).astype(BF16)
        gk_b = jnp.pad(w_gk[j, 1], ((GATE_RANK, LANES - 2 * GATE_RANK), (0, 0))).astype(BF16)
        return dict(norm_g=norm_g[l].reshape(1, D_MODEL), w_in=w_main, w_lr=w_lr, w_gk_f=gk_f, w_gk_b=gk_b,
                    b_gk_f=b_gk[j, 0].reshape(1, 256), b_gk_b=b_gk[j, 1].reshape(1, 256),
                    sgu_norm_g=sgu_norm_g[j].reshape(1, BRANCH_W), w_s=w_s[j].astype(BF16),
                    b_s=jnp.broadcast_to(b_s[j][:, :, None], (H_B, CHUNK_B, LANES)),
                    gla_norm_g=gla_norm_g[j].reshape(1, DV_A), w_out=w_out[l].astype(BF16),
                    tri_lo=tri_lo, tri_up=tri_up)

    def odd_weights(l):
        j = l // 2
        return dict(norm_g=norm_g[l].reshape(1, D_MODEL), w_in=w_in_odd[j].astype(BF16),
                    w_pool=w_pool[j].astype(BF16), pool_scale=pool_scale[j].reshape(1, BRANCH_W),
                    w_conv=w_conv[j], w_out=w_out[l].astype(BF16))

    weights = [even_weights(l) if l % 2 == 0 else odd_weights(l) for l in range(DEPTH)]

    def run_stream(x, batch, seq, mod_row_of, inits, latent):
        x = x.reshape(batch * seq, D_MODEL)
        finals = []
        for l in range(DEPTH):
            mod_row = functools.partial(mod_row_of, l)
            if l % 2 == 0:
                x, s_f, s_b = _even_layer(x, mod, mod_row, batch, seq, weights[l], *inits[l // 2])
                finals.append(jnp.stack([s_f, s_b], axis=1))
            else:
                last = l == DEPTH - 1
                x = _odd_layer(x, mod, mod_row, batch, seq, weights[l], colmajor_out=latent,
                               final_norm=last, final_norm_g=fng)
        return x.reshape(batch, seq, D_MODEL), finals

    zero_st = jnp.zeros((bp, H_A, LANES, LANES), F32)
    ctx_init = [(zero_st, zero_st)] * (DEPTH // 2)
    y_prompt, finals = run_stream(x_prompt, bp, tp, lambda l, b: l * MOD_ROWS, ctx_init, False)
    new_state = jnp.stack(finals, axis=1)

    lat_init = [(_state_to_kernel_layout(state_gla[:, j, 0]), _state_to_kernel_layout(state_gla[:, j, 1]))
                for j in range(DEPTH // 2)]
    y_sample, _ = run_stream(x_sample, bs, ts, lambda l, b: l * MOD_ROWS + 1 + b, lat_init, True)
    return (y_prompt, y_sample, new_state)
```

```python
import functools

import numpy as np
import jax
import jax.numpy as jnp
from jax import lax
from jax.experimental import pallas as pl
from jax.experimental.pallas import tpu as pltpu

F32 = jnp.float32
BF16 = jnp.bfloat16

D_MODEL = 1024
DEPTH = 4
GRID_W = 64
BRANCH_W = 512
H_A = 4
DK_A = 64
DV_A = 128
GATE_RANK = 16
GATE_NORMALIZER = 16.0
CHUNK_A = 64
H_B = 4
DH_B = 128
CHUNK_B = 128
POOL_WINDOWS = (2, 4, 8, 16)
G_C = 4
DG_C = 128
EPS = 1e-6

LANES = 128
TOKEN_BLOCK = 256
HALO = 8
MOD_ROWS = 8
VMEM_LIMIT = 56 * 1024 * 1024
N_PAIR = H_A // 2
PAIR_V = 2 * DV_A


def _dot(a, b):
    return jnp.dot(a, b, preferred_element_type=F32)


def _dot_nt(a, b):
    return lax.dot_general(a, b, (((1,), (1,)), ((), ())), preferred_element_type=F32)


def _rms(x, g):
    ms = jnp.mean(x * x, axis=-1, keepdims=True)
    return x * lax.rsqrt(ms + EPS) * g


def _silu(x):
    return x * (1.0 / (1.0 + jnp.exp(-x)))


def _log_sigmoid(x):
    return jnp.minimum(x, 0.0) - jnp.log1p(jnp.exp(-jnp.abs(x)))


def _split3_bf16(x):
    x1 = x.astype(BF16)
    r1 = x - x1.astype(F32)
    x2 = r1.astype(BF16)
    r2 = r1 - x2.astype(F32)
    x3 = r2.astype(BF16)
    return jnp.concatenate([x1, x2, x3], axis=1)


def _cumsum_blocks(tri, g):
    w = g.shape[1]
    bc = _dot(tri, _split3_bf16(g))
    return bc[:, 0:w] + bc[:, w:2 * w] + bc[:, 2 * w:3 * w]


def _log_decay(lr_bf16, wgk_ref, bgk_ref):
    gl = _dot(lr_bf16, wgk_ref[...]) + bgk_ref[...]
    return _log_sigmoid(gl) * (1.0 / GATE_NORMALIZER)


def _gla_block(q, k, b, vt_ref, st_scr, reverse, emit):
    tb = q.shape[0]
    nc = tb // CHUNK_A
    shift = CHUNK_A.bit_length() - 1

    def chunk_rows(idx):
        return jnp.concatenate(
            [jnp.broadcast_to(b[c * CHUNK_A + idx:c * CHUNK_A + idx + 1], (CHUNK_A, b.shape[1]))
             for c in range(nc)], axis=0)

    end_idx = 0 if reverse else CHUNK_A - 1
    b_end = chunk_rows(end_idx)
    b_mid = chunk_rows(CHUNK_A // 2)
    qs = q * jnp.exp(b)
    qa = q * jnp.exp(b - b_mid)
    ka = (k * jnp.exp(b_mid - b)).astype(BF16)
    kh = k * jnp.exp(b_end - b)
    dec = [jnp.exp(b[c * CHUNK_A + end_idx:c * CHUNK_A + end_idx + 1]) for c in range(nc)]

    row1 = lax.broadcasted_iota(jnp.int32, (tb, 1), 0)
    in_chunk = [jnp.right_shift(row1, shift) == c for c in range(nc)]
    r = lax.broadcasted_iota(jnp.int32, (tb, tb), 0)
    cc = lax.broadcasted_iota(jnp.int32, (tb, tb), 1)
    same_chunk = jnp.right_shift(r, shift) == jnp.right_shift(cc, shift)
    score_mask = jnp.logical_and(same_chunk, (cc >= r) if reverse else (cc <= r))
    lane = lax.broadcasted_iota(jnp.int32, (tb, LANES), 1)
    lane_h0 = lane < DK_A
    srow = lax.broadcasted_iota(jnp.int32, (PAIR_V, LANES), 0)
    slane = lax.broadcasted_iota(jnp.int32, (PAIR_V, LANES), 1)
    own_lanes = (srow < DV_A) == (slane < DK_A)
    zero_vt = jnp.zeros((DV_A, tb), BF16)
    order = range(nc - 1, -1, -1) if reverse else range(nc)

    for p in range(N_PAIR):
        sl = slice(p * LANES, (p + 1) * LANES)
        qa_p = qa[:, sl]
        lhs_a = jnp.concatenate([jnp.where(lane_h0, qa_p, 0.0), jnp.where(lane_h0, 0.0, qa_p)],
                                axis=0).astype(BF16)
        att2 = _dot_nt(lhs_a, ka[:, sl])
        att_h0 = jnp.where(score_mask, att2[0:tb], 0.0).astype(BF16)
        att_h1 = jnp.where(score_mask, att2[tb:2 * tb], 0.0).astype(BF16)

        kh_p = kh[:, sl]
        kh_routed = jnp.concatenate([jnp.where(in_chunk[c], kh_p, 0.0) for c in range(nc)],
                                    axis=1).astype(BF16)
        vt_p = vt_ref[p * PAIR_V:(p + 1) * PAIR_V, :]
        kv = _dot(vt_p, kh_routed)

        st = st_scr[p]
        entering = [None] * nc
        for c in order:
            entering[c] = st.astype(BF16)
            st = st * dec[c][:, sl] + jnp.where(own_lanes, kv[:, c * LANES:(c + 1) * LANES], 0.0)
        st_scr[p] = st

        qs_p = qs[:, sl]
        qs_routed = jnp.concatenate([jnp.where(in_chunk[c], qs_p, 0.0) for c in range(nc)],
                                    axis=1).astype(BF16)
        vt_diag = jnp.concatenate(
            [jnp.concatenate([vt_p[0:DV_A], zero_vt], axis=1),
             jnp.concatenate([zero_vt, vt_p[DV_A:PAIR_V]], axis=1)], axis=0)
        w_nt = jnp.concatenate([vt_diag] + entering, axis=1)
        lhs_o = jnp.concatenate([att_h0, att_h1, qs_routed], axis=1)
        emit(p, _dot_nt(lhs_o, w_nt))


def _write_final_state(st_scr, stout_ref):
    for h in range(H_A):
        hp = h % 2
        st_t = st_scr[h // 2, hp * DV_A:(hp + 1) * DV_A, :].T
        stout_ref[h] = st_t[hp * DK_A:(hp + 1) * DK_A, :]


def _even_fwd_kernel(x_ref, mod_ref, ng_ref, wf_ref, wvt_ref, wlr_ref, wgk_ref, bgk_ref, sng_ref, ws_ref,
                     bs_ref, tri_ref, st0_ref,
                     h_ref, qk_ref, vt_ref, lr_ref, of_ref, sp_ref, stout_ref, st_scr):
    t = pl.program_id(1)
    n_t = pl.num_programs(1)

    @pl.when(t == 0)
    def _():
        st_scr[...] = st0_ref[...]

    x = x_ref[...]
    shift = mod_ref[:, 0:D_MODEL]
    scale = mod_ref[:, D_MODEL:2 * D_MODEL]
    h = (_rms(x, ng_ref[...]) * (1.0 + scale) + shift).astype(BF16)
    h_ref[...] = h
    lr = _dot(h, wlr_ref[...])
    lr_ref[...] = lr
    g = _log_decay(lr.astype(BF16), wgk_ref, bgk_ref)
    qk = _dot(h, wf_ref[:, 0:512])
    q = qk[:, 0:256] * (DK_A ** -0.5)
    k = qk[:, 256:512]
    qk_ref[:, 0:256] = q
    qk_ref[:, 256:512] = k
    b = _cumsum_blocks(tri_ref[...], g)
    vt_ref[...] = _dot_nt(wvt_ref[...], h).astype(BF16)
    vs = _dot(h, wf_ref[:, 512:1024])

    def emit(p, o):
        of_ref[:, p * PAIR_V:(p + 1) * PAIR_V] = o

    _gla_block(q, k, b, vt_ref, st_scr, False, emit)

    vsn = _rms(vs, sng_ref[...]).astype(BF16)
    for n in range(x.shape[0] // CHUNK_B):
        rs = slice(n * CHUNK_B, (n + 1) * CHUNK_B)
        for hb in range(H_B):
            cs = slice(hb * DH_B, (hb + 1) * DH_B)
            sp_ref[rs, cs] = (_dot(ws_ref[hb], vsn[rs, cs]) + bs_ref[hb]).astype(BF16)

    @pl.when(t == n_t - 1)
    def _():
        _write_final_state(st_scr, stout_ref)


def _even_bwd_kernel(x_ref, mod_ref, h_ref, qk_ref, vt_ref, lr_ref, of_ref, sp_ref, wb_ref, wgk_ref, bgk_ref,
                     gng_ref, wout_ref, tri_ref, st0_ref,
                     xo_ref, stout_ref, st_scr, mix_scr):
    t = pl.program_id(1)
    n_t = pl.num_programs(1)

    @pl.when(t == 0)
    def _():
        st_scr[...] = st0_ref[...]

    h = h_ref[...]
    q = qk_ref[:, 0:256]
    k = qk_ref[:, 256:512]
    g = _log_decay(lr_ref[...].astype(BF16), wgk_ref, bgk_ref)
    ga = _dot(h, wb_ref[:, 0:512])
    b = _cumsum_blocks(tri_ref[...], g)
    ug = _dot(h, wb_ref[:, 512:1536])
    mix_scr[:, BRANCH_W:2 * BRANCH_W] = (ug[:, 0:512] * sp_ref[...].astype(F32)
                                         * _silu(ug[:, 512:1024])).astype(BF16)
    out_sgu = _dot(mix_scr[:, BRANCH_W:2 * BRANCH_W], wout_ref[BRANCH_W:2 * BRANCH_W, :])

    gng = gng_ref[...]

    def emit(p, o_b):
        for hp in range(2):
            cs = slice(p * PAIR_V + hp * DV_A, p * PAIR_V + (hp + 1) * DV_A)
            o = of_ref[:, cs] + o_b[:, hp * DV_A:(hp + 1) * DV_A]
            mix_scr[:, cs] = (_rms(o, gng) * _silu(ga[:, cs])).astype(BF16)

    _gla_block(q, k, b, vt_ref, st_scr, True, emit)

    out = _dot(mix_scr[:, 0:BRANCH_W], wout_ref[0:BRANCH_W, :]) + out_sgu
    gate = mod_ref[:, 2 * D_MODEL:3 * D_MODEL]
    xo_ref[...] = x_ref[...] + gate * out

    @pl.when(t == n_t - 1)
    def _():
        _write_final_state(st_scr, stout_ref)


def _odd_kernel(xp_ref, x_ref, xn_ref, mod_ref, ng_ref, win_ref, wpool_ref, pscale_ref, wconv_ref,
                wout_ref, fng_ref, xo_ref, xc_scr, u_scr, mix_scr, *, seq_len, colmajor_out, final_norm):
    t = pl.program_id(1)
    n_t = pl.num_programs(1)
    tb = x_ref.shape[0]

    x = x_ref[...]
    xe = jnp.concatenate([xp_ref[...], x, xn_ref[...]], axis=0)
    shift = mod_ref[:, 0:D_MODEL]
    scale = mod_ref[:, D_MODEL:2 * D_MODEL]
    h = (_rms(xe, ng_ref[...]) * (1.0 + scale) + shift).astype(BF16)
    z = _dot(h, win_ref[...])

    rowe = lax.broadcasted_iota(jnp.int32, (tb + 2 * HALO, 1), 0)
    valid = jnp.logical_and(jnp.logical_or(rowe >= HALO, t > 0),
                            jnp.logical_or(rowe < tb + HALO, t < n_t - 1))
    xc_scr[...] = jnp.where(valid, z[:, 0:512], 0.0)
    u_scr[...] = jnp.where(valid, z[:, 2048:2560] * z[:, 1024:1536], 0.0)

    pos = t * tb + lax.broadcasted_iota(jnp.int32, (tb, 1), 0)
    mid = slice(HALO, HALO + tb)
    for i, w in enumerate(POOL_WINDOWS):
        cs = slice(i * DG_C, (i + 1) * DG_C)
        s = xc_scr[pl.ds(HALO - w // 2, tb), cs]
        for off in range(-(w // 2) + 1, w - w // 2):
            s = s + xc_scr[pl.ds(HALO + off, tb), cs]
        lo = jnp.clip(pos - w // 2, 0, seq_len)
        hi = jnp.clip(pos + (w - w // 2), 0, seq_len)
        cnt = (hi - lo).astype(F32)
        pooled = s / cnt - xc_scr[mid, cs]
        pc = _dot(pooled.astype(BF16), wpool_ref[i])
        mix_scr[:, cs] = (pc * pscale_ref[:, cs] * _silu(z[mid, 512 + i * DG_C:512 + (i + 1) * DG_C])).astype(BF16)

    y = (u_scr[pl.ds(HALO - 1, tb), :] * wconv_ref[0:1, :] + u_scr[mid, :] * wconv_ref[1:2, :]
         + u_scr[pl.ds(HALO + 1, tb), :] * wconv_ref[2:3, :])
    mix_scr[:, BRANCH_W:2 * BRANCH_W] = (z[mid, 1536:2048] * y * _silu(z[mid, 2560:3072])).astype(BF16)

    out = _dot(mix_scr[...], wout_ref[...])
    gate = mod_ref[:, 2 * D_MODEL:3 * D_MODEL]
    xn = x + gate * out
    if final_norm:
        xn = _rms(xn, fng_ref[...])
    if colmajor_out:
        for j in range(tb // GRID_W):
            xo_ref[:, j * D_MODEL:(j + 1) * D_MODEL] = xn[j * GRID_W:(j + 1) * GRID_W]
    else:
        xo_ref[...] = xn


def _ada_kernel(c_ref, w_ref, b_ref, o_ref):
    c = c_ref[...]
    o_ref[...] = jnp.dot(_silu(c), w_ref[...], preferred_element_type=F32,
                         precision=lax.Precision.HIGHEST) + b_ref[...]


def _params(n_axes=2):
    return pltpu.CompilerParams(dimension_semantics=("arbitrary",) * n_axes,
                                vmem_limit_bytes=VMEM_LIMIT)


def _ada_mod(conds, w_ada, b_ada):
    n_col = 4
    cw = 3 * D_MODEL // n_col
    return pl.pallas_call(
        _ada_kernel,
        grid=(DEPTH, n_col),
        in_specs=[pl.BlockSpec((MOD_ROWS, D_MODEL), lambda l, j: (0, 0)),
                  pl.BlockSpec((None, D_MODEL, cw), lambda l, j: (l, 0, j)),
                  pl.BlockSpec((None, 1, cw), lambda l, j: (l, 0, j))],
        out_specs=pl.BlockSpec((None, MOD_ROWS, cw), lambda l, j: (l, 0, j)),
        out_shape=jax.ShapeDtypeStruct((DEPTH, MOD_ROWS, 3 * D_MODEL), F32),
        compiler_params=_params(2),
        name="ada_mod",
    )(conds, w_ada, b_ada.reshape(DEPTH, 1, 3 * D_MODEL))


def _tri_blocks(tb, upper):
    r = np.arange(tb)
    same = (r[:, None] // CHUNK_A) == (r[None, :] // CHUNK_A)
    tri = (r[None, :] >= r[:, None]) if upper else (r[None, :] <= r[:, None])
    return jnp.asarray((same & tri).astype(np.float32), dtype=BF16)


def _const_spec(shape):
    nd = len(shape)
    return pl.BlockSpec(shape, lambda b, t: (0,) * nd)


def _even_layer(x, mod, mod_row, batch, seq, lw, st0_f, st0_b):
    tb = TOKEN_BLOCK
    n_t = seq // tb
    n_tok = batch * seq
    n_blk = batch * n_t
    tok = lambda width: pl.BlockSpec((tb, width), lambda b, t: (b * n_t + t, 0))
    tok_rev = lambda width: pl.BlockSpec((tb, width), lambda b, t: (b * n_t + (n_t - 1 - t), 0))
    vt = pl.BlockSpec((N_PAIR * PAIR_V, tb), lambda b, t: (b * n_t + t, 0))
    vt_rev = pl.BlockSpec((N_PAIR * PAIR_V, tb), lambda b, t: (b * n_t + (n_t - 1 - t), 0))
    mod_spec = pl.BlockSpec((None, 1, 3 * D_MODEL), lambda b, t: (mod_row(b), 0, 0))
    st_in = pl.BlockSpec((None, N_PAIR, PAIR_V, LANES), lambda b, t: (b, 0, 0, 0))
    st_out = pl.BlockSpec((None, H_A, DK_A, DV_A), lambda b, t: (b, 0, 0, 0))
    st_shape = jax.ShapeDtypeStruct((batch, H_A, DK_A, DV_A), F32)
    st_scratch = pltpu.VMEM((N_PAIR, PAIR_V, LANES), F32)

    h, qk, v_t, lr, of, sp, s_f = pl.pallas_call(
        _even_fwd_kernel,
        grid=(batch, n_t),
        in_specs=[tok(D_MODEL), mod_spec, _const_spec((1, D_MODEL)),
                  _const_spec((D_MODEL, 1024)), _const_spec((BRANCH_W, D_MODEL)), _const_spec((D_MODEL, LANES)),
                  _const_spec((LANES, 256)), _const_spec((1, 256)), _const_spec((1, BRANCH_W)),
                  _const_spec((H_B, CHUNK_B, CHUNK_B)), _const_spec((H_B, CHUNK_B, LANES)),
                  _const_spec((tb, tb)), st_in],
        out_specs=[tok(D_MODEL), tok(512), vt, tok(LANES), tok(512), tok(512), st_out],
        out_shape=[jax.ShapeDtypeStruct((n_tok, D_MODEL), BF16), jax.ShapeDtypeStruct((n_tok, 512), F32),
                   jax.ShapeDtypeStruct((n_blk * N_PAIR * PAIR_V, tb), BF16),
                   jax.ShapeDtypeStruct((n_tok, LANES), F32),
                   jax.ShapeDtypeStruct((n_tok, 512), F32), jax.ShapeDtypeStruct((n_tok, 512), BF16),
                   st_shape],
        scratch_shapes=[st_scratch],
        compiler_params=_params(2),
        name="even_fwd",
    )(x, mod, lw["norm_g"], lw["w_f"], lw["w_vt"], lw["w_lr"], lw["w_gk_f"], lw["b_gk_f"], lw["sgu_norm_g"],
      lw["w_s"], lw["b_s"], lw["tri_lo"], st0_f)

    x_new, s_b = pl.pallas_call(
        _even_bwd_kernel,
        grid=(batch, n_t),
        in_specs=[tok_rev(D_MODEL), mod_spec, tok_rev(D_MODEL), tok_rev(512), vt_rev, tok_rev(LANES),
                  tok_rev(512), tok_rev(512), _const_spec((D_MODEL, 1536)), _const_spec((LANES, 256)),
                  _const_spec((1, 256)), _const_spec((1, DV_A)), _const_spec((D_MODEL, D_MODEL)),
                  _const_spec((tb, tb)), st_in],
        out_specs=[tok_rev(D_MODEL), st_out],
        out_shape=[jax.ShapeDtypeStruct((n_tok, D_MODEL), F32), st_shape],
        scratch_shapes=[st_scratch, pltpu.VMEM((tb, 2 * BRANCH_W), BF16)],
        compiler_params=_params(2),
        name="even_bwd",
    )(x, mod, h, qk, v_t, lr, of, sp, lw["w_b"], lw["w_gk_b"], lw["b_gk_b"], lw["gla_norm_g"], lw["w_out"],
      lw["tri_up"], st0_b)
    return x_new, s_f, s_b


def _odd_layer(x, mod, mod_row, batch, seq, lw, colmajor_out, final_norm, final_norm_g):
    tb = TOKEN_BLOCK
    n_t = seq // tb
    n_tok = batch * seq
    hb = tb // HALO
    n_halo_blocks = n_tok // HALO
    tok = pl.BlockSpec((tb, D_MODEL), lambda b, t: (b * n_t + t, 0))
    prev = pl.BlockSpec((HALO, D_MODEL), lambda b, t: (jnp.maximum((b * n_t + t) * hb - 1, 0), 0))
    nxt = pl.BlockSpec((HALO, D_MODEL),
                       lambda b, t: (jnp.minimum((b * n_t + t + 1) * hb, n_halo_blocks - 1), 0))
    mod_spec = pl.BlockSpec((None, 1, 3 * D_MODEL), lambda b, t: (mod_row(b), 0, 0))
    if colmajor_out:
        rows_per_block = tb // GRID_W
        out_spec = pl.BlockSpec((None, GRID_W, rows_per_block * D_MODEL), lambda b, t: (b, 0, t))
        out_shape = jax.ShapeDtypeStruct((batch, GRID_W, (seq // GRID_W) * D_MODEL), F32)
    else:
        out_spec = tok
        out_shape = jax.ShapeDtypeStruct((n_tok, D_MODEL), F32)
    kern = functools.partial(_odd_kernel, seq_len=seq, colmajor_out=colmajor_out, final_norm=final_norm)
    out = pl.pallas_call(
        kern,
        grid=(batch, n_t),
        in_specs=[prev, tok, nxt, mod_spec, _const_spec((1, D_MODEL)), _const_spec((D_MODEL, 3072)),
                  _const_spec((G_C, DG_C, DG_C)), _const_spec((1, BRANCH_W)), _const_spec((3, BRANCH_W)),
                  _const_spec((D_MODEL, D_MODEL)), _const_spec((1, D_MODEL))],
        out_specs=out_spec,
        out_shape=out_shape,
        scratch_shapes=[pltpu.VMEM((tb + 2 * HALO, BRANCH_W), F32), pltpu.VMEM((tb + 2 * HALO, BRANCH_W), F32),
                        pltpu.VMEM((tb, D_MODEL), BF16)],
        compiler_params=_params(2),
        name="odd_layer",
    )(x, x, x, mod, lw["norm_g"], lw["w_in"], lw["w_pool"], lw["pool_scale"], lw["w_conv"], lw["w_out"],
      final_norm_g)
    return out.reshape(n_tok, D_MODEL)


def _state_to_kernel_layout(s):
    st = jnp.swapaxes(s, -1, -2)
    parts = []
    for h in range(H_A):
        pad = ((0, 0), (0, 0), (DK_A, 0)) if h % 2 else ((0, 0), (0, 0), (0, DK_A))
        parts.append(jnp.pad(st[:, h], pad))
    return jnp.stack(parts, axis=1).reshape(s.shape[0], N_PAIR, PAIR_V, LANES)


def kernel(x_prompt, x_sample, c, state_gla, c_ctx, w_ada, b_ada, norm_g, w_in_even, w_in_odd, w_out, w_gk,
           b_gk, gla_norm_g, sgu_norm_g, w_s, b_s, w_pool, pool_scale, w_conv, final_norm_g):
    bp, tp, _ = x_prompt.shape
    bs, ts, _ = x_sample.shape
    assert tp % TOKEN_BLOCK == 0 and ts % TOKEN_BLOCK == 0 and ts == GRID_W * GRID_W
    assert 1 + bs <= MOD_ROWS

    conds = jnp.concatenate([c_ctx[None], c, jnp.zeros((MOD_ROWS - 1 - bs, D_MODEL), F32)], axis=0)
    mod = _ada_mod(conds, w_ada, b_ada).reshape(DEPTH * MOD_ROWS, 1, 3 * D_MODEL)

    tri_lo = _tri_blocks(TOKEN_BLOCK, upper=False)
    tri_up = _tri_blocks(TOKEN_BLOCK, upper=True)
    fng = final_norm_g.reshape(1, D_MODEL)

    def even_weights(l):
        j = l // 2
        w = w_in_even[j]
        w_f = jnp.concatenate([w[:, 0:512], w[:, 2080:2592]], axis=1).astype(BF16)
        w_vt = w[:, 512:1024].T.astype(BF16)
        w_b = jnp.concatenate([w[:, 1024:1536], w[:, 1568:2080], w[:, 2592:3104]], axis=1).astype(BF16)
        w_lr = jnp.pad(w[:, 1536:1568], ((0, 0), (0, LANES - 2 * GATE_RANK))).astype(BF16)
        gk_f = jnp.pad(w_gk[j, 0], ((0, LANES - GATE_RANK), (0, 0))).astype(BF16)
        gk_b = jnp.pad(w_gk[j, 1], ((GATE_RANK, LANES - 2 * GATE_RANK), (0, 0))).astype(BF16)
        return dict(norm_g=norm_g[l].reshape(1, D_MODEL), w_f=w_f, w_vt=w_vt, w_b=w_b, w_lr=w_lr,
                    w_gk_f=gk_f, w_gk_b=gk_b,
                    b_gk_f=b_gk[j, 0].reshape(1, 256), b_gk_b=b_gk[j, 1].reshape(1, 256),
                    sgu_norm_g=sgu_norm_g[j].reshape(1, BRANCH_W), w_s=w_s[j].astype(BF16),
                    b_s=jnp.broadcast_to(b_s[j][:, :, None], (H_B, CHUNK_B, LANES)),
                    gla_norm_g=gla_norm_g[j].reshape(1, DV_A), w_out=w_out[l].astype(BF16),
                    tri_lo=tri_lo, tri_up=tri_up)

    def odd_weights(l):
        j = l // 2
        return dict(norm_g=norm_g[l].reshape(1, D_MODEL), w_in=w_in_odd[j].astype(BF16),
                    w_pool=w_pool[j].astype(BF16), pool_scale=pool_scale[j].reshape(1, BRANCH_W),
                    w_conv=w_conv[j], w_out=w_out[l].astype(BF16))

    weights = [even_weights(l) if l % 2 == 0 else odd_weights(l) for l in range(DEPTH)]

    def run_stream(x, batch, seq, mod_row_of, inits, latent):
        x = x.reshape(batch * seq, D_MODEL)
        finals = []
        for l in range(DEPTH):
            mod_row = functools.partial(mod_row_of, l)
            if l % 2 == 0:
                x, s_f, s_b = _even_layer(x, mod, mod_row, batch, seq, weights[l], *inits[l // 2])
                finals.append(jnp.stack([s_f, s_b], axis=1))
            else:
                last = l == DEPTH - 1
                x = _odd_layer(x, mod, mod_row, batch, seq, weights[l], colmajor_out=latent,
                               final_norm=last, final_norm_g=fng)
        return x.reshape(batch, seq, D_MODEL), finals

    zero_st = jnp.zeros((bp, N_PAIR, PAIR_V, LANES), F32)
    ctx_init = [(zero_st, zero_st)] * (DEPTH // 2)
    y_prompt, finals = run_stream(x_prompt, bp, tp, lambda l, b: l * MOD_ROWS, ctx_init, False)
    new_state = jnp.stack(finals, axis=1)

    lat_init = [(_state_to_kernel_layout(state_gla[:, j, 0]), _state_to_kernel_layout(state_gla[:, j, 1]))
                for j in range(DEPTH // 2)]
    y_sample, _ = run_stream(x_sample, bs, ts, lambda l, b: l * MOD_ROWS + 1 + b, lat_init, True)
    return (y_prompt, y_sample, new_state)
```

```python
import functools

import numpy as np
import jax
import jax.numpy as jnp
from jax import lax
from jax.experimental import pallas as pl
from jax.experimental.pallas import tpu as pltpu

F32 = jnp.float32
BF16 = jnp.bfloat16

D_MODEL = 1024
DEPTH = 4
GRID_W = 64
BRANCH_W = 512
H_A = 4
DK_A = 64
DV_A = 128
GATE_RANK = 16
GATE_NORMALIZER = 16.0
CHUNK_A = 64
H_B = 4
DH_B = 128
CHUNK_B = 128
POOL_WINDOWS = (2, 4, 8, 16)
G_C = 4
DG_C = 128
EPS = 1e-6

LANES = 128
TOKEN_BLOCK = 256
COLMAJOR_TOKEN_BLOCK = 512
HALO = 8
MOD_ROWS = 16
VMEM_LIMIT = 56 * 1024 * 1024
N_PAIR = H_A // 2
PAIR_V = 2 * DV_A


def _dot(a, b):
    return jnp.dot(a, b, preferred_element_type=F32)


def _dot_nt(a, b):
    return lax.dot_general(a, b, (((1,), (1,)), ((), ())), preferred_element_type=F32)


def _rms(x, g):
    ms = jnp.mean(x * x, axis=-1, keepdims=True)
    return x * lax.rsqrt(ms + EPS) * g


def _silu(x):
    return x * (1.0 / (1.0 + jnp.exp(-x)))


def _log_sigmoid(x):
    return jnp.minimum(x, 0.0) - jnp.log1p(jnp.exp(-jnp.abs(x)))


def _split2_bf16(x):
    x1 = x.astype(BF16)
    x2 = (x - x1.astype(F32)).astype(BF16)
    return jnp.concatenate([x1, x2], axis=1)


def _cumsum_blocks(tri, g):
    w = g.shape[1]
    bc = _dot(tri, _split2_bf16(g))
    return bc[:, 0:w] + bc[:, w:2 * w]


def _log_decay(lr_bf16, wgk_ref, bgk_ref):
    gl = _dot(lr_bf16, wgk_ref[...]) + bgk_ref[...]
    return _log_sigmoid(gl) * (1.0 / GATE_NORMALIZER)


def _gla_block(q, k, b, vt_ref, st_scr, reverse, emit, fill=(lambda: None, lambda: None)):
    tb = q.shape[0]
    nc = tb // CHUNK_A
    shift = CHUNK_A.bit_length() - 1

    def chunk_rows(idx):
        return jnp.concatenate(
            [jnp.broadcast_to(b[c * CHUNK_A + idx:c * CHUNK_A + idx + 1], (CHUNK_A, b.shape[1]))
             for c in range(nc)], axis=0)

    end_idx = 0 if reverse else CHUNK_A - 1
    b_end = chunk_rows(end_idx)
    b_mid = chunk_rows(CHUNK_A // 2)
    qs = q * jnp.exp(b)
    qa = q * jnp.exp(b - b_mid)
    ka = (k * jnp.exp(b_mid - b)).astype(BF16)
    kh = k * jnp.exp(b_end - b)
    dec = [jnp.exp(b[c * CHUNK_A + end_idx:c * CHUNK_A + end_idx + 1]) for c in range(nc)]

    row1 = lax.broadcasted_iota(jnp.int32, (tb, 1), 0)
    in_chunk = [jnp.right_shift(row1, shift) == c for c in range(nc)]
    r = lax.broadcasted_iota(jnp.int32, (tb, tb), 0)
    cc = lax.broadcasted_iota(jnp.int32, (tb, tb), 1)
    same_chunk = jnp.right_shift(r, shift) == jnp.right_shift(cc, shift)
    score_mask = jnp.logical_and(same_chunk, (cc >= r) if reverse else (cc <= r))
    lane = lax.broadcasted_iota(jnp.int32, (tb, LANES), 1)
    lane_h0 = lane < DK_A
    srow = lax.broadcasted_iota(jnp.int32, (PAIR_V, LANES), 0)
    slane = lax.broadcasted_iota(jnp.int32, (PAIR_V, LANES), 1)
    own_lanes = (srow < DV_A) == (slane < DK_A)
    zero_vt = jnp.zeros((DV_A, tb), BF16)
    order = range(nc - 1, -1, -1) if reverse else range(nc)

    pairs = [slice(p * LANES, (p + 1) * LANES) for p in range(N_PAIR)]
    vts = [vt_ref[p * PAIR_V:(p + 1) * PAIR_V, :] for p in range(N_PAIR)]

    att = []
    for sl in pairs:
        qa_p = qa[:, sl]
        lhs_a = jnp.concatenate([jnp.where(lane_h0, qa_p, 0.0), jnp.where(lane_h0, 0.0, qa_p)],
                                axis=0).astype(BF16)
        att2 = _dot_nt(lhs_a, ka[:, sl])
        att.append((jnp.where(score_mask, att2[0:tb], 0.0).astype(BF16),
                    jnp.where(score_mask, att2[tb:2 * tb], 0.0).astype(BF16)))
    fill[0]()

    entering = []
    for p, sl in enumerate(pairs):
        kh_p = kh[:, sl]
        kh_routed = jnp.concatenate([jnp.where(in_chunk[c], kh_p, 0.0) for c in range(nc)],
                                    axis=1).astype(BF16)
        kv = _dot(vts[p], kh_routed)
        st = st_scr[p]
        ent = [None] * nc
        for c in order:
            ent[c] = st.astype(BF16)
            st = st * dec[c][:, sl] + jnp.where(own_lanes, kv[:, c * LANES:(c + 1) * LANES], 0.0)
        st_scr[p] = st
        entering.append(ent)
    fill[1]()

    for p, sl in enumerate(pairs):
        qs_p = qs[:, sl]
        qs_routed = jnp.concatenate([jnp.where(in_chunk[c], qs_p, 0.0) for c in range(nc)],
                                    axis=1).astype(BF16)
        vt_p = vts[p]
        vt_diag = jnp.concatenate(
            [jnp.concatenate([vt_p[0:DV_A], zero_vt], axis=1),
             jnp.concatenate([zero_vt, vt_p[DV_A:PAIR_V]], axis=1)], axis=0)
        w_nt = jnp.concatenate([vt_diag] + entering[p], axis=1)
        lhs_o = jnp.concatenate([att[p][0], att[p][1], qs_routed], axis=1)
        emit(p, _dot_nt(lhs_o, w_nt))


def _write_final_state(st_scr, stout_ref):
    for h in range(H_A):
        hp = h % 2
        st_t = st_scr[h // 2, hp * DV_A:(hp + 1) * DV_A, :].T
        stout_ref[h] = st_t[hp * DK_A:(hp + 1) * DK_A, :]


def _even_fwd_kernel(x_ref, mod_ref, ng_ref, wf_ref, wvt_ref, wlr_ref, wgk_ref, bgk_ref, sng_ref, ws_ref,
                     bs_ref, tri_ref, st0_ref,
                     h_ref, qk_ref, vt_ref, lr_ref, of_ref, sp_ref, stout_ref, st_scr, vs_scr):
    t = pl.program_id(1)
    n_t = pl.num_programs(1)

    @pl.when(t == 0)
    def _():
        st_scr[...] = st0_ref[...]

    x = x_ref[...]
    shift = mod_ref[:, 0:D_MODEL]
    scale = mod_ref[:, D_MODEL:2 * D_MODEL]
    h = (_rms(x, ng_ref[...]) * (1.0 + scale) + shift).astype(BF16)
    h_ref[...] = h
    lr = _dot(h, wlr_ref[...])
    lr_ref[...] = lr
    g = _log_decay(lr.astype(BF16), wgk_ref, bgk_ref)
    qk = _dot(h, wf_ref[:, 0:512])
    q = qk[:, 0:256] * (DK_A ** -0.5)
    k = qk[:, 256:512]
    qk_ref[:, 0:256] = q
    qk_ref[:, 256:512] = k
    b = _cumsum_blocks(tri_ref[...], g)
    vt_ref[...] = _dot_nt(wvt_ref[...], h).astype(BF16)

    def emit(p, o):
        of_ref[:, p * PAIR_V:(p + 1) * PAIR_V] = o

    def sgu_input():
        vs = _dot(h, wf_ref[:, 512:1024])
        vs_scr[...] = _rms(vs, sng_ref[...]).astype(BF16)

    def sgu_mix():
        for n in range(x.shape[0] // CHUNK_B):
            rs = slice(n * CHUNK_B, (n + 1) * CHUNK_B)
            for hb in range(H_B):
                cs = slice(hb * DH_B, (hb + 1) * DH_B)
                sp_ref[rs, cs] = (_dot(ws_ref[hb], vs_scr[rs, cs]) + bs_ref[hb]).astype(BF16)

    _gla_block(q, k, b, vt_ref, st_scr, False, emit, (sgu_input, sgu_mix))

    @pl.when(t == n_t - 1)
    def _():
        _write_final_state(st_scr, stout_ref)


def _even_bwd_kernel(x_ref, mod_ref, h_ref, qk_ref, vt_ref, lr_ref, of_ref, sp_ref, wb_ref, wgk_ref, bgk_ref,
                     gng_ref, wout_ref, tri_ref, st0_ref,
                     xo_ref, stout_ref, st_scr, mix_scr, outs_scr):
    t = pl.program_id(1)
    n_t = pl.num_programs(1)

    @pl.when(t == 0)
    def _():
        st_scr[...] = st0_ref[...]

    h = h_ref[...]
    q = qk_ref[:, 0:256]
    k = qk_ref[:, 256:512]
    g = _log_decay(lr_ref[...].astype(BF16), wgk_ref, bgk_ref)
    ga = _dot(h, wb_ref[:, 0:512])
    b = _cumsum_blocks(tri_ref[...], g)

    def sgu_gate():
        ug = _dot(h, wb_ref[:, 512:1536])
        mix_scr[:, BRANCH_W:2 * BRANCH_W] = (ug[:, 0:512] * sp_ref[...].astype(F32)
                                             * _silu(ug[:, 512:1024])).astype(BF16)

    def sgu_out():
        outs_scr[...] = _dot(mix_scr[:, BRANCH_W:2 * BRANCH_W], wout_ref[BRANCH_W:2 * BRANCH_W, :])

    gng = gng_ref[...]

    def emit(p, o_b):
        for hp in range(2):
            cs = slice(p * PAIR_V + hp * DV_A, p * PAIR_V + (hp + 1) * DV_A)
            o = of_ref[:, cs] + o_b[:, hp * DV_A:(hp + 1) * DV_A]
            mix_scr[:, cs] = (_rms(o, gng) * _silu(ga[:, cs])).astype(BF16)

    _gla_block(q, k, b, vt_ref, st_scr, True, emit, (sgu_gate, sgu_out))

    out = _dot(mix_scr[:, 0:BRANCH_W], wout_ref[0:BRANCH_W, :]) + outs_scr[...]
    gate = mod_ref[:, 2 * D_MODEL:3 * D_MODEL]
    xo_ref[...] = x_ref[...] + gate * out

    @pl.when(t == n_t - 1)
    def _():
        _write_final_state(st_scr, stout_ref)


def _odd_kernel(xp_ref, x_ref, xn_ref, mod_ref, ng_ref, win_ref, wpool_ref, pscale_ref, wconv_ref,
                wout_ref, fng_ref, xo_ref, xc_scr, u_scr, mix_scr, *, seq_len, colmajor_out, final_norm):
    t = pl.program_id(1)
    n_t = pl.num_programs(1)
    tb = x_ref.shape[0]

    x = x_ref[...]
    xe = jnp.concatenate([xp_ref[...], x, xn_ref[...]], axis=0)
    shift = mod_ref[:, 0:D_MODEL]
    scale = mod_ref[:, D_MODEL:2 * D_MODEL]
    h = (_rms(xe, ng_ref[...]) * (1.0 + scale) + shift).astype(BF16)
    z = _dot(h, win_ref[...])

    rowe = lax.broadcasted_iota(jnp.int32, (tb + 2 * HALO, 1), 0)
    valid = jnp.logical_and(jnp.logical_or(rowe >= HALO, t > 0),
                            jnp.logical_or(rowe < tb + HALO, t < n_t - 1))
    xc_scr[...] = jnp.where(valid, z[:, 0:512], 0.0)
    u_scr[...] = jnp.where(valid, z[:, 2048:2560] * z[:, 1024:1536], 0.0)

    pos = t * tb + lax.broadcasted_iota(jnp.int32, (tb, 1), 0)
    mid = slice(HALO, HALO + tb)
    for i, w in enumerate(POOL_WINDOWS):
        cs = slice(i * DG_C, (i + 1) * DG_C)
        s = xc_scr[pl.ds(HALO - w // 2, tb), cs]
        for off in range(-(w // 2) + 1, w - w // 2):
            s = s + xc_scr[pl.ds(HALO + off, tb), cs]
        lo = jnp.clip(pos - w // 2, 0, seq_len)
        hi = jnp.clip(pos + (w - w // 2), 0, seq_len)
        cnt = (hi - lo).astype(F32)
        pooled = s / cnt - xc_scr[mid, cs]
        pc = _dot(pooled.astype(BF16), wpool_ref[i])
        mix_scr[:, cs] = (pc * pscale_ref[:, cs] * _silu(z[mid, 512 + i * DG_C:512 + (i + 1) * DG_C])).astype(BF16)

    y = (u_scr[pl.ds(HALO - 1, tb), :] * wconv_ref[0:1, :] + u_scr[mid, :] * wconv_ref[1:2, :]
         + u_scr[pl.ds(HALO + 1, tb), :] * wconv_ref[2:3, :])
    mix_scr[:, BRANCH_W:2 * BRANCH_W] = (z[mid, 1536:2048] * y * _silu(z[mid, 2560:3072])).astype(BF16)

    out = _dot(mix_scr[...], wout_ref[...])
    gate = mod_ref[:, 2 * D_MODEL:3 * D_MODEL]
    xn = x + gate * out
    if final_norm:
        xn = _rms(xn, fng_ref[...])
    if colmajor_out:
        for j in range(tb // GRID_W):
            xo_ref[:, j, :] = xn[j * GRID_W:(j + 1) * GRID_W]
    else:
        xo_ref[...] = xn


def _ada_kernel(c_ref, w_ref, b_ref, o_ref):
    o_ref[...] = _dot(_silu(c_ref[...]).astype(BF16), w_ref[...].astype(BF16)) + b_ref[...]


def _params(n_axes=2):
    return pltpu.CompilerParams(dimension_semantics=("arbitrary",) * n_axes,
                                vmem_limit_bytes=VMEM_LIMIT)


def _ada_mod(conds, w_ada, b_ada):
    n_col = 2
    cw = 3 * D_MODEL // n_col
    return pl.pallas_call(
        _ada_kernel,
        grid=(DEPTH, n_col),
        in_specs=[pl.BlockSpec((MOD_ROWS, D_MODEL), lambda l, j: (0, 0)),
                  pl.BlockSpec((None, D_MODEL, cw), lambda l, j: (l, 0, j)),
                  pl.BlockSpec((None, 1, cw), lambda l, j: (l, 0, j))],
        out_specs=pl.BlockSpec((None, MOD_ROWS, cw), lambda l, j: (l, 0, j)),
        out_shape=jax.ShapeDtypeStruct((DEPTH, MOD_ROWS, 3 * D_MODEL), F32),
        compiler_params=_params(2),
        name="ada_mod",
    )(conds, w_ada, b_ada.reshape(DEPTH, 1, 3 * D_MODEL))


def _tri_blocks(tb, upper):
    r = np.arange(tb)
    same = (r[:, None] // CHUNK_A) == (r[None, :] // CHUNK_A)
    tri = (r[None, :] >= r[:, None]) if upper else (r[None, :] <= r[:, None])
    return jnp.asarray((same & tri).astype(np.float32), dtype=BF16)


def _const_spec(shape):
    nd = len(shape)
    return pl.BlockSpec(shape, lambda b, t: (0,) * nd)


def _even_layer(x, mod, mod_row, batch, seq, lw, st0_f, st0_b):
    tb = TOKEN_BLOCK
    n_t = seq // tb
    n_tok = batch * seq
    n_blk = batch * n_t
    tok = lambda width: pl.BlockSpec((tb, width), lambda b, t: (b * n_t + t, 0))
    tok_rev = lambda width: pl.BlockSpec((tb, width), lambda b, t: (b * n_t + (n_t - 1 - t), 0))
    vt = pl.BlockSpec((N_PAIR * PAIR_V, tb), lambda b, t: (b * n_t + t, 0))
    vt_rev = pl.BlockSpec((N_PAIR * PAIR_V, tb), lambda b, t: (b * n_t + (n_t - 1 - t), 0))
    mod_spec = pl.BlockSpec((None, 1, 3 * D_MODEL), lambda b, t: (mod_row(b), 0, 0))
    st_in = pl.BlockSpec((None, N_PAIR, PAIR_V, LANES), lambda b, t: (b, 0, 0, 0))
    st_out = pl.BlockSpec((None, H_A, DK_A, DV_A), lambda b, t: (b, 0, 0, 0))
    st_shape = jax.ShapeDtypeStruct((batch, H_A, DK_A, DV_A), F32)
    st_scratch = pltpu.VMEM((N_PAIR, PAIR_V, LANES), F32)

    h, qk, v_t, lr, of, sp, s_f = pl.pallas_call(
        _even_fwd_kernel,
        grid=(batch, n_t),
        in_specs=[tok(D_MODEL), mod_spec, _const_spec((1, D_MODEL)),
                  _const_spec((D_MODEL, 1024)), _const_spec((BRANCH_W, D_MODEL)), _const_spec((D_MODEL, LANES)),
                  _const_spec((LANES, 256)), _const_spec((1, 256)), _const_spec((1, BRANCH_W)),
                  _const_spec((H_B, CHUNK_B, CHUNK_B)), _const_spec((H_B, CHUNK_B, LANES)),
                  _const_spec((tb, tb)), st_in],
        out_specs=[tok(D_MODEL), tok(512), vt, tok(LANES), tok(512), tok(512), st_out],
        out_shape=[jax.ShapeDtypeStruct((n_tok, D_MODEL), BF16), jax.ShapeDtypeStruct((n_tok, 512), F32),
                   jax.ShapeDtypeStruct((n_blk * N_PAIR * PAIR_V, tb), BF16),
                   jax.ShapeDtypeStruct((n_tok, LANES), F32),
                   jax.ShapeDtypeStruct((n_tok, 512), F32), jax.ShapeDtypeStruct((n_tok, 512), BF16),
                   st_shape],
        scratch_shapes=[st_scratch, pltpu.VMEM((tb, BRANCH_W), BF16)],
        compiler_params=_params(2),
        name="even_fwd",
    )(x, mod, lw["norm_g"], lw["w_f"], lw["w_vt"], lw["w_lr"], lw["w_gk_f"], lw["b_gk_f"], lw["sgu_norm_g"],
      lw["w_s"], lw["b_s"], lw["tri_lo"], st0_f)

    x_new, s_b = pl.pallas_call(
        _even_bwd_kernel,
        grid=(batch, n_t),
        in_specs=[tok_rev(D_MODEL), mod_spec, tok_rev(D_MODEL), tok_rev(512), vt_rev, tok_rev(LANES),
                  tok_rev(512), tok_rev(512), _const_spec((D_MODEL, 1536)), _const_spec((LANES, 256)),
                  _const_spec((1, 256)), _const_spec((1, DV_A)), _const_spec((D_MODEL, D_MODEL)),
                  _const_spec((tb, tb)), st_in],
        out_specs=[tok_rev(D_MODEL), st_out],
        out_shape=[jax.ShapeDtypeStruct((n_tok, D_MODEL), F32), st_shape],
        scratch_shapes=[st_scratch, pltpu.VMEM((tb, 2 * BRANCH_W), BF16), pltpu.VMEM((tb, D_MODEL), F32)],
        compiler_params=_params(2),
        name="even_bwd",
    )(x, mod, h, qk, v_t, lr, of, sp, lw["w_b"], lw["w_gk_b"], lw["b_gk_b"], lw["gla_norm_g"], lw["w_out"],
      lw["tri_up"], st0_b)
    return x_new, s_f, s_b


def _odd_layer(x, mod, mod_row, batch, seq, lw, colmajor_out, final_norm, final_norm_g):
    tb = COLMAJOR_TOKEN_BLOCK if colmajor_out else TOKEN_BLOCK
    n_t = seq // tb
    n_tok = batch * seq
    hb = tb // HALO
    n_halo_blocks = n_tok // HALO
    tok = pl.BlockSpec((tb, D_MODEL), lambda b, t: (b * n_t + t, 0))
    prev = pl.BlockSpec((HALO, D_MODEL), lambda b, t: (jnp.maximum((b * n_t + t) * hb - 1, 0), 0))
    nxt = pl.BlockSpec((HALO, D_MODEL),
                       lambda b, t: (jnp.minimum((b * n_t + t + 1) * hb, n_halo_blocks - 1), 0))
    mod_spec = pl.BlockSpec((None, 1, 3 * D_MODEL), lambda b, t: (mod_row(b), 0, 0))
    if colmajor_out:
        rows_per_block = tb // GRID_W
        out_spec = pl.BlockSpec((None, GRID_W, rows_per_block, D_MODEL), lambda b, t: (b, 0, t, 0))
        out_shape = jax.ShapeDtypeStruct((batch, GRID_W, seq // GRID_W, D_MODEL), F32)
    else:
        out_spec = tok
        out_shape = jax.ShapeDtypeStruct((n_tok, D_MODEL), F32)
    kern = functools.partial(_odd_kernel, seq_len=seq, colmajor_out=colmajor_out, final_norm=final_norm)
    out = pl.pallas_call(
        kern,
        grid=(batch, n_t),
        in_specs=[prev, tok, nxt, mod_spec, _const_spec((1, D_MODEL)), _const_spec((D_MODEL, 3072)),
                  _const_spec((G_C, DG_C, DG_C)), _const_spec((1, BRANCH_W)), _const_spec((3, BRANCH_W)),
                  _const_spec((D_MODEL, D_MODEL)), _const_spec((1, D_MODEL))],
        out_specs=out_spec,
        out_shape=out_shape,
        scratch_shapes=[pltpu.VMEM((tb + 2 * HALO, BRANCH_W), F32), pltpu.VMEM((tb + 2 * HALO, BRANCH_W), F32),
                        pltpu.VMEM((tb, D_MODEL), BF16)],
        compiler_params=_params(2),
        name="odd_layer",
    )(x, x, x, mod, lw["norm_g"], lw["w_in"], lw["w_pool"], lw["pool_scale"], lw["w_conv"], lw["w_out"],
      final_norm_g)
    return out.reshape(n_tok, D_MODEL)


def _state_to_kernel_layout(s):
    st = jnp.swapaxes(s, -1, -2)
    parts = []
    for h in range(H_A):
        pad = ((0, 0), (0, 0), (DK_A, 0)) if h % 2 else ((0, 0), (0, 0), (0, DK_A))
        parts.append(jnp.pad(st[:, h], pad))
    return jnp.stack(parts, axis=1).reshape(s.shape[0], N_PAIR, PAIR_V, LANES)


def kernel(x_prompt, x_sample, c, state_gla, c_ctx, w_ada, b_ada, norm_g, w_in_even, w_in_odd, w_out, w_gk,
           b_gk, gla_norm_g, sgu_norm_g, w_s, b_s, w_pool, pool_scale, w_conv, final_norm_g):
    bp, tp, _ = x_prompt.shape
    bs, ts, _ = x_sample.shape
    assert tp % TOKEN_BLOCK == 0 and ts % TOKEN_BLOCK == 0 and ts == GRID_W * GRID_W
    assert 1 + bs <= MOD_ROWS

    conds = jnp.concatenate([c_ctx[None], c, jnp.zeros((MOD_ROWS - 1 - bs, D_MODEL), F32)], axis=0)
    mod = _ada_mod(conds, w_ada, b_ada).reshape(DEPTH * MOD_ROWS, 1, 3 * D_MODEL)

    tri_lo = _tri_blocks(TOKEN_BLOCK, upper=False)
    tri_up = _tri_blocks(TOKEN_BLOCK, upper=True)
    fng = final_norm_g.reshape(1, D_MODEL)

    def even_weights(l):
        j = l // 2
        w = w_in_even[j]
        w_f = jnp.concatenate([w[:, 0:512], w[:, 2080:2592]], axis=1).astype(BF16)
        w_vt = w[:, 512:1024].T.astype(BF16)
        w_b = jnp.concatenate([w[:, 1024:1536], w[:, 1568:2080], w[:, 2592:3104]], axis=1).astype(BF16)
        w_lr = jnp.pad(w[:, 1536:1568], ((0, 0), (0, LANES - 2 * GATE_RANK))).astype(BF16)
        gk_f = jnp.pad(w_gk[j, 0], ((0, LANES - GATE_RANK), (0, 0))).astype(BF16)
        gk_b = jnp.pad(w_gk[j, 1], ((GATE_RANK, LANES - 2 * GATE_RANK), (0, 0))).astype(BF16)
        return dict(norm_g=norm_g[l].reshape(1, D_MODEL), w_f=w_f, w_vt=w_vt, w_b=w_b, w_lr=w_lr,
                    w_gk_f=gk_f, w_gk_b=gk_b,
                    b_gk_f=b_gk[j, 0].reshape(1, 256), b_gk_b=b_gk[j, 1].reshape(1, 256),
                    sgu_norm_g=sgu_norm_g[j].reshape(1, BRANCH_W), w_s=w_s[j].astype(BF16),
                    b_s=jnp.broadcast_to(b_s[j][:, :, None], (H_B, CHUNK_B, LANES)),
                    gla_norm_g=gla_norm_g[j].reshape(1, DV_A), w_out=w_out[l].astype(BF16),
                    tri_lo=tri_lo, tri_up=tri_up)

    def odd_weights(l):
        j = l // 2
        return dict(norm_g=norm_g[l].reshape(1, D_MODEL), w_in=w_in_odd[j].astype(BF16),
                    w_pool=w_pool[j].astype(BF16), pool_scale=pool_scale[j].reshape(1, BRANCH_W),
                    w_conv=w_conv[j], w_out=w_out[l].astype(BF16))

    weights = [even_weights(l) if l % 2 == 0 else odd_weights(l) for l in range(DEPTH)]

    def run_stream(x, batch, seq, mod_row_of, inits, latent):
        x = x.reshape(batch * seq, D_MODEL)
        finals = []
        for l in range(DEPTH):
            mod_row = functools.partial(mod_row_of, l)
            if l % 2 == 0:
                x, s_f, s_b = _even_layer(x, mod, mod_row, batch, seq, weights[l], *inits[l // 2])
                finals.append(jnp.stack([s_f, s_b], axis=1))
            else:
                last = l == DEPTH - 1
                x = _odd_layer(x, mod, mod_row, batch, seq, weights[l], colmajor_out=latent,
                               final_norm=last, final_norm_g=fng)
        return x.reshape(batch, seq, D_MODEL), finals

    zero_st = jnp.zeros((bp, N_PAIR, PAIR_V, LANES), F32)
    ctx_init = [(zero_st, zero_st)] * (DEPTH // 2)
    y_prompt, finals = run_stream(x_prompt, bp, tp, lambda l, b: l * MOD_ROWS, ctx_init, False)
    new_state = jnp.stack(finals, axis=1)

    lat_init = [(_state_to_kernel_layout(state_gla[:, j, 0]), _state_to_kernel_layout(state_gla[:, j, 1]))
                for j in range(DEPTH // 2)]
    y_sample, _ = run_stream(x_sample, bs, ts, lambda l, b: l * MOD_ROWS + 1 + b, lat_init, True)
    return (y_prompt, y_sample, new_state)
```

```python
import functools

import numpy as np
import jax
import jax.numpy as jnp
from jax import lax
from jax.experimental import pallas as pl
from jax.experimental.pallas import tpu as pltpu

F32 = jnp.float32
BF16 = jnp.bfloat16

D_MODEL = 1024
DEPTH = 4
GRID_W = 64
BRANCH_W = 512
H_A = 4
DK_A = 64
DV_A = 128
GATE_RANK = 16
GATE_NORMALIZER = 16.0
CHUNK_A = 64
H_B = 4
DH_B = 128
CHUNK_B = 128
POOL_WINDOWS = (2, 4, 8, 16)
G_C = 4
DG_C = 128
EPS = 1e-6

LANES = 128
TOKEN_BLOCK = 256
GLA_BLOCK = 256
EVEN_TOKEN_BLOCK = 512
COLMAJOR_TOKEN_BLOCK = 512
ODD_SUB_BLOCKS = 2
HALO = 8
MOD_ROWS = 16
VMEM_LIMIT = 56 * 1024 * 1024
N_PAIR = H_A // 2
PAIR_V = 2 * DV_A


def _dot(a, b):
    return jnp.dot(a, b, preferred_element_type=F32)


def _dot_nt(a, b):
    return lax.dot_general(a, b, (((1,), (1,)), ((), ())), preferred_element_type=F32)


def _rms(x, g):
    ms = jnp.mean(x * x, axis=-1, keepdims=True)
    return x * lax.rsqrt(ms + EPS) * g


def _silu(x):
    return x * (1.0 / (1.0 + jnp.exp(-x)))


def _log_sigmoid(x):
    return jnp.minimum(x, 0.0) - jnp.log1p(jnp.exp(-jnp.abs(x)))


def _split2_bf16(x):
    x1 = x.astype(BF16)
    x2 = (x - x1.astype(F32)).astype(BF16)
    return jnp.concatenate([x1, x2], axis=1)


def _cumsum_blocks(tri, g):
    w = g.shape[1]
    bc = _dot(tri, _split2_bf16(g))
    return bc[:, 0:w] + bc[:, w:2 * w]


def _log_decay(lr_bf16, wgk_ref, bgk_ref):
    gl = _dot(lr_bf16, wgk_ref[...]) + bgk_ref[...]
    return _log_sigmoid(gl) * (1.0 / GATE_NORMALIZER)


def _gla_block(q, k, b, vt_ref, vt_cols, st_scr, reverse, emit, fill=(lambda: None, lambda: None)):
    tb = q.shape[0]
    nc = tb // CHUNK_A
    shift = CHUNK_A.bit_length() - 1

    def chunk_rows(idx):
        return jnp.concatenate(
            [jnp.broadcast_to(b[c * CHUNK_A + idx:c * CHUNK_A + idx + 1], (CHUNK_A, b.shape[1]))
             for c in range(nc)], axis=0)

    end_idx = 0 if reverse else CHUNK_A - 1
    b_end = chunk_rows(end_idx)
    b_mid = chunk_rows(CHUNK_A // 2)
    qs = q * jnp.exp(b)
    qa = q * jnp.exp(b - b_mid)
    ka = (k * jnp.exp(b_mid - b)).astype(BF16)
    kh = k * jnp.exp(b_end - b)
    dec = [jnp.exp(b[c * CHUNK_A + end_idx:c * CHUNK_A + end_idx + 1]) for c in range(nc)]

    row1 = lax.broadcasted_iota(jnp.int32, (tb, 1), 0)
    in_chunk = [jnp.right_shift(row1, shift) == c for c in range(nc)]
    r = lax.broadcasted_iota(jnp.int32, (tb, tb), 0)
    cc = lax.broadcasted_iota(jnp.int32, (tb, tb), 1)
    same_chunk = jnp.right_shift(r, shift) == jnp.right_shift(cc, shift)
    score_mask = jnp.logical_and(same_chunk, (cc >= r) if reverse else (cc <= r))
    lane = lax.broadcasted_iota(jnp.int32, (tb, LANES), 1)
    lane_h0 = lane < DK_A
    srow = lax.broadcasted_iota(jnp.int32, (PAIR_V, LANES), 0)
    slane = lax.broadcasted_iota(jnp.int32, (PAIR_V, LANES), 1)
    own_lanes = (srow < DV_A) == (slane < DK_A)
    zero_vt = jnp.zeros((DV_A, tb), BF16)
    order = range(nc - 1, -1, -1) if reverse else range(nc)

    pairs = [slice(p * LANES, (p + 1) * LANES) for p in range(N_PAIR)]
    vts = [vt_ref[p * PAIR_V:(p + 1) * PAIR_V, vt_cols] for p in range(N_PAIR)]

    att = []
    for sl in pairs:
        qa_p = qa[:, sl]
        lhs_a = jnp.concatenate([jnp.where(lane_h0, qa_p, 0.0), jnp.where(lane_h0, 0.0, qa_p)],
                                axis=0).astype(BF16)
        att2 = _dot_nt(lhs_a, ka[:, sl])
        att.append((jnp.where(score_mask, att2[0:tb], 0.0).astype(BF16),
                    jnp.where(score_mask, att2[tb:2 * tb], 0.0).astype(BF16)))
    fill[0]()

    entering = []
    for p, sl in enumerate(pairs):
        kh_p = kh[:, sl]
        kh_routed = jnp.concatenate([jnp.where(in_chunk[c], kh_p, 0.0) for c in range(nc)],
                                    axis=1).astype(BF16)
        kv = _dot(vts[p], kh_routed)
        st = st_scr[p]
        ent = [None] * nc
        for c in order:
            ent[c] = st.astype(BF16)
            st = st * dec[c][:, sl] + jnp.where(own_lanes, kv[:, c * LANES:(c + 1) * LANES], 0.0)
        st_scr[p] = st
        entering.append(ent)
    fill[1]()

    for p, sl in enumerate(pairs):
        qs_p = qs[:, sl]
        qs_routed = jnp.concatenate([jnp.where(in_chunk[c], qs_p, 0.0) for c in range(nc)],
                                    axis=1).astype(BF16)
        vt_p = vts[p]
        vt_diag = jnp.concatenate(
            [jnp.concatenate([vt_p[0:DV_A], zero_vt], axis=1),
             jnp.concatenate([zero_vt, vt_p[DV_A:PAIR_V]], axis=1)], axis=0)
        w_nt = jnp.concatenate([vt_diag] + entering[p], axis=1)
        lhs_o = jnp.concatenate([att[p][0], att[p][1], qs_routed], axis=1)
        emit(p, _dot_nt(lhs_o, w_nt))


def _write_final_state(st_scr, stout_ref, slot):
    for h in range(H_A):
        hp = h % 2
        st_t = st_scr[h // 2, hp * DV_A:(hp + 1) * DV_A, :].T
        stout_ref[slot, h] = st_t[hp * DK_A:(hp + 1) * DK_A, :]


def _even_fwd_kernel(x_ref, mod_ref, ng_ref, wf_ref, wvt_ref, wlr_ref, wgk_ref, bgk_ref, sng_ref, ws_ref,
                     bs_ref, tri_ref, st0_ref,
                     h_ref, qk_ref, vt_ref, lr_ref, of_ref, sp_ref, stout_ref, st_scr, vs_scr, *, carry):
    t = pl.program_id(1)
    n_t = pl.num_programs(1)
    tb = x_ref.shape[0]
    n_sub = tb // GLA_BLOCK

    if carry:
        @pl.when(t == 0)
        def _():
            st_scr[...] = st0_ref[0]

    x = x_ref[...]
    shift = mod_ref[:, 0:D_MODEL]
    scale = mod_ref[:, D_MODEL:2 * D_MODEL]
    h = (_rms(x, ng_ref[...]) * (1.0 + scale) + shift).astype(BF16)
    h_ref[...] = h
    lr = _dot(h, wlr_ref[...])
    lr_ref[...] = lr
    g = _log_decay(lr.astype(BF16), wgk_ref, bgk_ref)
    qk = _dot(h, wf_ref[:, 0:512])
    q = qk[:, 0:256] * (DK_A ** -0.5)
    k = qk[:, 256:512]
    qk_ref[:, 0:256] = q
    qk_ref[:, 256:512] = k
    tri = tri_ref[...]
    b = [_cumsum_blocks(tri, g[s * GLA_BLOCK:(s + 1) * GLA_BLOCK]) for s in range(n_sub)]
    vt_ref[...] = _dot_nt(wvt_ref[...], h).astype(BF16)

    def sgu_input():
        vs = _dot(h, wf_ref[:, 512:1024])
        vs_scr[...] = _rms(vs, sng_ref[...]).astype(BF16)

    def sgu_mix():
        for n in range(tb // CHUNK_B):
            rs = slice(n * CHUNK_B, (n + 1) * CHUNK_B)
            for hb in range(H_B):
                cs = slice(hb * DH_B, (hb + 1) * DH_B)
                sp_ref[rs, cs] = (_dot(ws_ref[hb], vs_scr[rs, cs]) + bs_ref[hb]).astype(BF16)

    for s in range(n_sub):
        rows = slice(s * GLA_BLOCK, (s + 1) * GLA_BLOCK)
        if not carry:
            st_scr[...] = st0_ref[s]

        def emit(p, o, rows=rows):
            of_ref[rows, p * PAIR_V:(p + 1) * PAIR_V] = o

        fill = (sgu_input, sgu_mix) if s == 0 else (lambda: None, lambda: None)
        _gla_block(q[rows], k[rows], b[s], vt_ref, rows, st_scr, False, emit, fill)
        if not carry:
            _write_final_state(st_scr, stout_ref, s)

    if carry:
        @pl.when(t == n_t - 1)
        def _():
            _write_final_state(st_scr, stout_ref, 0)


def _even_bwd_kernel(x_ref, mod_ref, h_ref, qk_ref, vt_ref, lr_ref, of_ref, sp_ref, wb_ref, wgk_ref, bgk_ref,
                     gng_ref, wout_ref, tri_ref, st0_ref,
                     xo_ref, stout_ref, st_scr, mix_scr, outs_scr, *, carry):
    t = pl.program_id(1)
    n_t = pl.num_programs(1)
    tb = x_ref.shape[0]
    n_sub = tb // GLA_BLOCK

    if carry:
        @pl.when(t == 0)
        def _():
            st_scr[...] = st0_ref[0]

    h = h_ref[...]
    g = _log_decay(lr_ref[...].astype(BF16), wgk_ref, bgk_ref)
    ga = _dot(h, wb_ref[:, 0:512])
    tri = tri_ref[...]
    b = [_cumsum_blocks(tri, g[s * GLA_BLOCK:(s + 1) * GLA_BLOCK]) for s in range(n_sub)]

    def sgu_gate():
        ug = _dot(h, wb_ref[:, 512:1536])
        mix_scr[:, BRANCH_W:2 * BRANCH_W] = (ug[:, 0:512] * sp_ref[...].astype(F32)
                                             * _silu(ug[:, 512:1024])).astype(BF16)

    def sgu_out():
        outs_scr[...] = _dot(mix_scr[:, BRANCH_W:2 * BRANCH_W], wout_ref[BRANCH_W:2 * BRANCH_W, :])

    gng = gng_ref[...]
    order = range(n_sub - 1, -1, -1) if carry else range(n_sub)
    for i, s in enumerate(order):
        rows = slice(s * GLA_BLOCK, (s + 1) * GLA_BLOCK)
        if not carry:
            st_scr[...] = st0_ref[s]

        def emit(p, o_b, rows=rows):
            for hp in range(2):
                cs = slice(p * PAIR_V + hp * DV_A, p * PAIR_V + (hp + 1) * DV_A)
                o = of_ref[rows, cs] + o_b[:, hp * DV_A:(hp + 1) * DV_A]
                mix_scr[rows, cs] = (_rms(o, gng) * _silu(ga[rows, cs])).astype(BF16)

        fill = (sgu_gate, sgu_out) if i == 0 else (lambda: None, lambda: None)
        _gla_block(qk_ref[rows, 0:256], qk_ref[rows, 256:512], b[s], vt_ref, rows, st_scr, True, emit, fill)
        if not carry:
            _write_final_state(st_scr, stout_ref, s)

    out = _dot(mix_scr[:, 0:BRANCH_W], wout_ref[0:BRANCH_W, :]) + outs_scr[...]
    gate = mod_ref[:, 2 * D_MODEL:3 * D_MODEL]
    xo_ref[...] = x_ref[...] + gate * out

    if carry:
        @pl.when(t == n_t - 1)
        def _():
            _write_final_state(st_scr, stout_ref, 0)


def _window_sum(a, w):
    n = a.shape[0]
    ahead = lambda v, k: pltpu.roll(v, n - k, 0)
    behind = lambda v, k: pltpu.roll(v, k, 0)
    half = w // 2
    p, span = a, 1
    while span < half:
        p = p + ahead(p, span)
        span *= 2
    return (behind(p, half) + p)[HALO:n - HALO]


def _odd_kernel(xp_ref, x_ref, xn_ref, mod_ref, ng_ref, win_ref, wpool_ref, pscale_ref, wconv_ref,
                wout_ref, fng_ref, xo_ref, mix_scr, *, seq_len, n_sub, colmajor_out, final_norm):
    t = pl.program_id(1)
    n_t = pl.num_programs(1)
    tb = x_ref.shape[0]
    sb = tb // n_sub
    ne = sb + 2 * HALO

    shift = mod_ref[:, 0:D_MODEL]
    scale = mod_ref[:, D_MODEL:2 * D_MODEL]
    gate = mod_ref[:, 2 * D_MODEL:3 * D_MODEL]
    ng = ng_ref[...]
    xe = jnp.concatenate([xp_ref[...], x_ref[...], xn_ref[...]], axis=0)

    bounds = [0] + [ne + s * sb for s in range(n_sub)]
    zp = []
    for s in range(n_sub):
        h = (_rms(xe[bounds[s]:bounds[s + 1]], ng) * (1.0 + scale) + shift).astype(BF16)
        zp.append(_dot(h, win_ref[...]))

    mid = slice(HALO, HALO + sb)
    for s in range(n_sub):
        r0 = s * sb
        z = zp[0] if s == 0 else jnp.concatenate([zp[s - 1][-2 * HALO:], zp[s]], axis=0)
        rowe = r0 + lax.broadcasted_iota(jnp.int32, (ne, 1), 0)
        valid = jnp.logical_and(jnp.logical_or(rowe >= HALO, t > 0),
                                jnp.logical_or(rowe < tb + HALO, t < n_t - 1))
        xc = jnp.where(valid, z[:, 0:512], 0.0)
        u = jnp.where(valid, z[:, 2048:2560] * z[:, 1024:1536], 0.0)

        pos = t * tb + r0 + lax.broadcasted_iota(jnp.int32, (sb, 1), 0)
        pooled = []
        for i, w in enumerate(POOL_WINDOWS):
            cs = slice(i * DG_C, (i + 1) * DG_C)
            lo = jnp.clip(pos - w // 2, 0, seq_len)
            hi = jnp.clip(pos + (w - w // 2), 0, seq_len)
            inv_cnt = 1.0 / (hi - lo).astype(F32)
            pooled.append(_window_sum(xc[:, cs], w) * inv_cnt - xc[mid, cs])
        pc = _dot(jnp.concatenate(pooled, axis=1).astype(BF16), wpool_ref[...])
        mix_scr[r0:r0 + sb, 0:BRANCH_W] = (pc * pscale_ref[...] * _silu(z[mid, 512:1024])).astype(BF16)

        y = (pltpu.roll(u, 1, 0) * wconv_ref[0:1, :] + u * wconv_ref[1:2, :]
             + pltpu.roll(u, ne - 1, 0) * wconv_ref[2:3, :])
        mix_scr[r0:r0 + sb, BRANCH_W:2 * BRANCH_W] = (z[mid, 1536:2048] * y[mid]
                                                      * _silu(z[mid, 2560:3072])).astype(BF16)

        out = _dot(mix_scr[r0:r0 + sb, :], wout_ref[...])
        xn = xe[HALO + r0:HALO + r0 + sb] + gate * out
        if final_norm:
            xn = _rms(xn, fng_ref[...])
        if colmajor_out:
            for j in range(sb // GRID_W):
                xo_ref[:, r0 // GRID_W + j, :] = xn[j * GRID_W:(j + 1) * GRID_W]
        else:
            xo_ref[r0:r0 + sb, :] = xn


def _ada_kernel(c_ref, w_ref, b_ref, o_ref):
    o_ref[...] = _dot(_silu(c_ref[...]).astype(BF16), w_ref[...].astype(BF16)) + b_ref[...]


def _params(n_axes=2):
    return pltpu.CompilerParams(dimension_semantics=("arbitrary",) * n_axes,
                                vmem_limit_bytes=VMEM_LIMIT)


def _ada_mod(conds, w_ada, b_ada):
    n_col = 2
    cw = 3 * D_MODEL // n_col
    return pl.pallas_call(
        _ada_kernel,
        grid=(DEPTH, n_col),
        in_specs=[pl.BlockSpec((MOD_ROWS, D_MODEL), lambda l, j: (0, 0)),
                  pl.BlockSpec((None, D_MODEL, cw), lambda l, j: (l, 0, j)),
                  pl.BlockSpec((None, 1, cw), lambda l, j: (l, 0, j))],
        out_specs=pl.BlockSpec((None, MOD_ROWS, cw), lambda l, j: (l, 0, j)),
        out_shape=jax.ShapeDtypeStruct((DEPTH, MOD_ROWS, 3 * D_MODEL), F32),
        compiler_params=_params(2),
        name="ada_mod",
    )(conds, w_ada, b_ada.reshape(DEPTH, 1, 3 * D_MODEL))


def _tri_blocks(tb, upper):
    r = np.arange(tb)
    same = (r[:, None] // CHUNK_A) == (r[None, :] // CHUNK_A)
    tri = (r[None, :] >= r[:, None]) if upper else (r[None, :] <= r[:, None])
    return jnp.asarray((same & tri).astype(np.float32), dtype=BF16)


def _const_spec(shape):
    nd = len(shape)
    return pl.BlockSpec(shape, lambda b, t: (0,) * nd)


def _even_layer(x, mod, mod_row, batch, seq, lw, st0_f, st0_b):
    tb = EVEN_TOKEN_BLOCK
    carry = seq >= tb
    spp = 1 if carry else tb // seq
    assert seq % tb == 0 if carry else (seq == GLA_BLOCK and batch % spp == 0)
    n_t = seq * spp // tb
    n_grp = batch // spp
    n_tok = batch * seq
    n_blk = n_grp * n_t
    tok = lambda width: pl.BlockSpec((tb, width), lambda b, t: (b * n_t + t, 0))
    tok_rev = lambda width: pl.BlockSpec((tb, width), lambda b, t: (b * n_t + (n_t - 1 - t), 0))
    vt = pl.BlockSpec((N_PAIR * PAIR_V, tb), lambda b, t: (b * n_t + t, 0))
    vt_rev = pl.BlockSpec((N_PAIR * PAIR_V, tb), lambda b, t: (b * n_t + (n_t - 1 - t), 0))
    mod_spec = pl.BlockSpec((None, 1, 3 * D_MODEL), lambda b, t: (mod_row(b * spp), 0, 0))
    st_in = pl.BlockSpec((spp, N_PAIR, PAIR_V, LANES), lambda b, t: (b, 0, 0, 0))
    st_out = pl.BlockSpec((spp, H_A, DK_A, DV_A), lambda b, t: (b, 0, 0, 0))
    st_shape = jax.ShapeDtypeStruct((batch, H_A, DK_A, DV_A), F32)
    st_scratch = pltpu.VMEM((N_PAIR, PAIR_V, LANES), F32)
    gb = GLA_BLOCK

    h, qk, v_t, lr, of, sp, s_f = pl.pallas_call(
        functools.partial(_even_fwd_kernel, carry=carry),
        grid=(n_grp, n_t),
        in_specs=[tok(D_MODEL), mod_spec, _const_spec((1, D_MODEL)),
                  _const_spec((D_MODEL, 1024)), _const_spec((BRANCH_W, D_MODEL)), _const_spec((D_MODEL, LANES)),
                  _const_spec((LANES, 256)), _const_spec((1, 256)), _const_spec((1, BRANCH_W)),
                  _const_spec((H_B, CHUNK_B, CHUNK_B)), _const_spec((H_B, CHUNK_B, LANES)),
                  _const_spec((gb, gb)), st_in],
        out_specs=[tok(D_MODEL), tok(512), vt, tok(LANES), tok(512), tok(512), st_out],
        out_shape=[jax.ShapeDtypeStruct((n_tok, D_MODEL), BF16), jax.ShapeDtypeStruct((n_tok, 512), F32),
                   jax.ShapeDtypeStruct((n_blk * N_PAIR * PAIR_V, tb), BF16),
                   jax.ShapeDtypeStruct((n_tok, LANES), F32),
                   jax.ShapeDtypeStruct((n_tok, 512), F32), jax.ShapeDtypeStruct((n_tok, 512), BF16),
                   st_shape],
        scratch_shapes=[st_scratch, pltpu.VMEM((tb, BRANCH_W), BF16)],
        compiler_params=_params(2),
        name="even_fwd",
    )(x, mod, lw["norm_g"], lw["w_f"], lw["w_vt"], lw["w_lr"], lw["w_gk_f"], lw["b_gk_f"], lw["sgu_norm_g"],
      lw["w_s"], lw["b_s"], lw["tri_lo"], st0_f)

    x_new, s_b = pl.pallas_call(
        functools.partial(_even_bwd_kernel, carry=carry),
        grid=(n_grp, n_t),
        in_specs=[tok_rev(D_MODEL), mod_spec, tok_rev(D_MODEL), tok_rev(512), vt_rev, tok_rev(LANES),
                  tok_rev(512), tok_rev(512), _const_spec((D_MODEL, 1536)), _const_spec((LANES, 256)),
                  _const_spec((1, 256)), _const_spec((1, DV_A)), _const_spec((D_MODEL, D_MODEL)),
                  _const_spec((gb, gb)), st_in],
        out_specs=[tok_rev(D_MODEL), st_out],
        out_shape=[jax.ShapeDtypeStruct((n_tok, D_MODEL), F32), st_shape],
        scratch_shapes=[st_scratch, pltpu.VMEM((tb, 2 * BRANCH_W), BF16), pltpu.VMEM((tb, D_MODEL), F32)],
        compiler_params=_params(2),
        name="even_bwd",
    )(x, mod, h, qk, v_t, lr, of, sp, lw["w_b"], lw["w_gk_b"], lw["b_gk_b"], lw["gla_norm_g"], lw["w_out"],
      lw["tri_up"], st0_b)
    return x_new, s_f, s_b


def _odd_layer(x, mod, mod_row, batch, seq, lw, colmajor_out, final_norm, final_norm_g):
    tb = COLMAJOR_TOKEN_BLOCK if colmajor_out else TOKEN_BLOCK
    n_t = seq // tb
    n_tok = batch * seq
    hb = tb // HALO
    n_halo_blocks = n_tok // HALO
    tok = pl.BlockSpec((tb, D_MODEL), lambda b, t: (b * n_t + t, 0))
    prev = pl.BlockSpec((HALO, D_MODEL), lambda b, t: (jnp.maximum((b * n_t + t) * hb - 1, 0), 0))
    nxt = pl.BlockSpec((HALO, D_MODEL),
                       lambda b, t: (jnp.minimum((b * n_t + t + 1) * hb, n_halo_blocks - 1), 0))
    mod_spec = pl.BlockSpec((None, 1, 3 * D_MODEL), lambda b, t: (mod_row(b), 0, 0))
    if colmajor_out:
        rows_per_block = tb // GRID_W
        out_spec = pl.BlockSpec((None, GRID_W, rows_per_block, D_MODEL), lambda b, t: (b, 0, t, 0))
        out_shape = jax.ShapeDtypeStruct((batch, GRID_W, seq // GRID_W, D_MODEL), F32)
    else:
        out_spec = tok
        out_shape = jax.ShapeDtypeStruct((n_tok, D_MODEL), F32)
    kern = functools.partial(_odd_kernel, seq_len=seq, n_sub=ODD_SUB_BLOCKS, colmajor_out=colmajor_out,
                             final_norm=final_norm)
    out = pl.pallas_call(
        kern,
        grid=(batch, n_t),
        in_specs=[prev, tok, nxt, mod_spec, _const_spec((1, D_MODEL)), _const_spec((D_MODEL, 3072)),
                  _const_spec((BRANCH_W, BRANCH_W)), _const_spec((1, BRANCH_W)), _const_spec((3, BRANCH_W)),
                  _const_spec((D_MODEL, D_MODEL)), _const_spec((1, D_MODEL))],
        out_specs=out_spec,
        out_shape=out_shape,
        scratch_shapes=[pltpu.VMEM((tb, D_MODEL), BF16)],
        compiler_params=_params(2),
        name="odd_layer",
    )(x, x, x, mod, lw["norm_g"], lw["w_in"], lw["w_pool"], lw["pool_scale"], lw["w_conv"], lw["w_out"],
      final_norm_g)
    return out.reshape(n_tok, D_MODEL)


def _state_to_kernel_layout(s):
    st = jnp.swapaxes(s, -1, -2)
    parts = []
    for h in range(H_A):
        pad = ((0, 0), (0, 0), (DK_A, 0)) if h % 2 else ((0, 0), (0, 0), (0, DK_A))
        parts.append(jnp.pad(st[:, h], pad))
    return jnp.stack(parts, axis=1).reshape(s.shape[0], N_PAIR, PAIR_V, LANES)


def kernel(x_prompt, x_sample, c, state_gla, c_ctx, w_ada, b_ada, norm_g, w_in_even, w_in_odd, w_out, w_gk,
           b_gk, gla_norm_g, sgu_norm_g, w_s, b_s, w_pool, pool_scale, w_conv, final_norm_g):
    bp, tp, _ = x_prompt.shape
    bs, ts, _ = x_sample.shape
    assert tp == TOKEN_BLOCK and ts == GRID_W * GRID_W and ts % COLMAJOR_TOKEN_BLOCK == 0
    assert 1 + bs <= MOD_ROWS

    conds = jnp.concatenate([c_ctx[None], c, jnp.zeros((MOD_ROWS - 1 - bs, D_MODEL), F32)], axis=0)
    mod = _ada_mod(conds, w_ada, b_ada).reshape(DEPTH * MOD_ROWS, 1, 3 * D_MODEL)

    tri_lo = _tri_blocks(GLA_BLOCK, upper=False)
    tri_up = _tri_blocks(GLA_BLOCK, upper=True)
    fng = final_norm_g.reshape(1, D_MODEL)

    def even_weights(l):
        j = l // 2
        w = w_in_even[j]
        w_f = jnp.concatenate([w[:, 0:512], w[:, 2080:2592]], axis=1).astype(BF16)
        w_vt = w[:, 512:1024].T.astype(BF16)
        w_b = jnp.concatenate([w[:, 1024:1536], w[:, 1568:2080], w[:, 2592:3104]], axis=1).astype(BF16)
        w_lr = jnp.pad(w[:, 1536:1568], ((0, 0), (0, LANES - 2 * GATE_RANK))).astype(BF16)
        gk_f = jnp.pad(w_gk[j, 0], ((0, LANES - GATE_RANK), (0, 0))).astype(BF16)
        gk_b = jnp.pad(w_gk[j, 1], ((GATE_RANK, LANES - 2 * GATE_RANK), (0, 0))).astype(BF16)
        return dict(norm_g=norm_g[l].reshape(1, D_MODEL), w_f=w_f, w_vt=w_vt, w_b=w_b, w_lr=w_lr,
                    w_gk_f=gk_f, w_gk_b=gk_b,
                    b_gk_f=b_gk[j, 0].reshape(1, 256), b_gk_b=b_gk[j, 1].reshape(1, 256),
                    sgu_norm_g=sgu_norm_g[j].reshape(1, BRANCH_W), w_s=w_s[j].astype(BF16),
                    b_s=jnp.broadcast_to(b_s[j][:, :, None], (H_B, CHUNK_B, LANES)),
                    gla_norm_g=gla_norm_g[j].reshape(1, DV_A), w_out=w_out[l].astype(BF16),
                    tri_lo=tri_lo, tri_up=tri_up)

    def odd_weights(l):
        j = l // 2
        return dict(norm_g=norm_g[l].reshape(1, D_MODEL), w_in=w_in_odd[j].astype(BF16),
                    w_pool=jax.scipy.linalg.block_diag(*w_pool[j]).astype(BF16), pool_scale=pool_scale[j].reshape(1, BRANCH_W),
                    w_conv=w_conv[j], w_out=w_out[l].astype(BF16))

    weights = [even_weights(l) if l % 2 == 0 else odd_weights(l) for l in range(DEPTH)]

    def run_stream(x, batch, seq, mod_row_of, inits, latent):
        x = x.reshape(batch * seq, D_MODEL)
        finals = []
        for l in range(DEPTH):
            mod_row = functools.partial(mod_row_of, l)
            if l % 2 == 0:
                x, s_f, s_b = _even_layer(x, mod, mod_row, batch, seq, weights[l], *inits[l // 2])
                finals.append(jnp.stack([s_f, s_b], axis=1))
            else:
                last = l == DEPTH - 1
                x = _odd_layer(x, mod, mod_row, batch, seq, weights[l], colmajor_out=latent,
                               final_norm=last, final_norm_g=fng)
        return x.reshape(batch, seq, D_MODEL), finals

    zero_st = jnp.zeros((bp, N_PAIR, PAIR_V, LANES), F32)
    ctx_init = [(zero_st, zero_st)] * (DEPTH // 2)
    y_prompt, finals = run_stream(x_prompt, bp, tp, lambda l, b: l * MOD_ROWS, ctx_init, False)
    new_state = jnp.stack(finals, axis=1)

    lat_init = [(_state_to_kernel_layout(state_gla[:, j, 0]), _state_to_kernel_layout(state_gla[:, j, 1]))
                for j in range(DEPTH // 2)]
    y_sample, _ = run_stream(x_sample, bs, ts, lambda l, b: l * MOD_ROWS + 1 + b, lat_init, True)
    return (y_prompt, y_sample, new_state)
```

```python
import functools

import numpy as np
import jax
import jax.numpy as jnp
from jax import lax
from jax.experimental import pallas as pl
from jax.experimental.pallas import tpu as pltpu

F32 = jnp.float32
BF16 = jnp.bfloat16

D_MODEL = 1024
DEPTH = 4
GRID_W = 64
BRANCH_W = 512
H_A = 4
DK_A = 64
DV_A = 128
GATE_RANK = 16
GATE_NORMALIZER = 16.0
CHUNK_A = 64
H_B = 4
DH_B = 128
CHUNK_B = 128
POOL_WINDOWS = (2, 4, 8, 16)
G_C = 4
DG_C = 128
EPS = 1e-6

LANES = 128
GLA_BLOCK = 256
EVEN_TOKEN_BLOCK = 1024
ODD_TOKEN_BLOCK = 1024
ODD_SLAB = 256
HALO = 8
MOD_ROWS = 16
VMEM_LIMIT = 56 * 1024 * 1024
N_PAIR = H_A // 2
PAIR_V = 2 * DV_A


def _dot(a, b):
    return jnp.dot(a, b, preferred_element_type=F32)


def _dot_nt(a, b):
    return lax.dot_general(a, b, (((1,), (1,)), ((), ())), preferred_element_type=F32)


def _rms(x, g):
    ms = jnp.mean(x * x, axis=-1, keepdims=True)
    return x * lax.rsqrt(ms + EPS) * g


def _silu(x):
    return x * (1.0 / (1.0 + jnp.exp(-x)))


def _log_sigmoid(x):
    return jnp.minimum(x, 0.0) - jnp.log1p(jnp.exp(-jnp.abs(x)))


def _split2_bf16(x):
    x1 = x.astype(BF16)
    x2 = (x - x1.astype(F32)).astype(BF16)
    return jnp.concatenate([x1, x2], axis=1)


def _cumsum_blocks(tri, g):
    w = g.shape[1]
    bc = _dot(tri, _split2_bf16(g))
    return bc[:, 0:w] + bc[:, w:2 * w]


def _log_decay(lr_bf16, wgk_ref, bgk_ref):
    gl = _dot(lr_bf16, wgk_ref[...]) + bgk_ref[...]
    return _log_sigmoid(gl) * (1.0 / GATE_NORMALIZER)


def _gla_block(q, k, b, vt_ref, vt_cols, st_scr, reverse, emit, fill=(lambda: None, lambda: None)):
    tb = q.shape[0]
    nc = tb // CHUNK_A
    shift = CHUNK_A.bit_length() - 1

    def chunk_rows(idx):
        return jnp.concatenate(
            [jnp.broadcast_to(b[c * CHUNK_A + idx:c * CHUNK_A + idx + 1], (CHUNK_A, b.shape[1]))
             for c in range(nc)], axis=0)

    end_idx = 0 if reverse else CHUNK_A - 1
    b_end = chunk_rows(end_idx)
    b_mid = chunk_rows(CHUNK_A // 2)
    qs = q * jnp.exp(b)
    qa = q * jnp.exp(b - b_mid)
    ka = (k * jnp.exp(b_mid - b)).astype(BF16)
    kh = k * jnp.exp(b_end - b)
    dec = [jnp.exp(b[c * CHUNK_A + end_idx:c * CHUNK_A + end_idx + 1]) for c in range(nc)]

    row1 = lax.broadcasted_iota(jnp.int32, (tb, 1), 0)
    in_chunk = [jnp.right_shift(row1, shift) == c for c in range(nc)]
    r = lax.broadcasted_iota(jnp.int32, (tb, tb), 0)
    cc = lax.broadcasted_iota(jnp.int32, (tb, tb), 1)
    same_chunk = jnp.right_shift(r, shift) == jnp.right_shift(cc, shift)
    score_mask = jnp.logical_and(same_chunk, (cc >= r) if reverse else (cc <= r))
    lane = lax.broadcasted_iota(jnp.int32, (tb, LANES), 1)
    lane_h0 = lane < DK_A
    srow = lax.broadcasted_iota(jnp.int32, (PAIR_V, LANES), 0)
    slane = lax.broadcasted_iota(jnp.int32, (PAIR_V, LANES), 1)
    own_lanes = (srow < DV_A) == (slane < DK_A)
    zero_vt = jnp.zeros((DV_A, tb), BF16)
    order = range(nc - 1, -1, -1) if reverse else range(nc)

    pairs = [slice(p * LANES, (p + 1) * LANES) for p in range(N_PAIR)]
    vts = [vt_ref[p * PAIR_V:(p + 1) * PAIR_V, vt_cols] for p in range(N_PAIR)]

    att = []
    for sl in pairs:
        qa_p = qa[:, sl]
        lhs_a = jnp.concatenate([jnp.where(lane_h0, qa_p, 0.0), jnp.where(lane_h0, 0.0, qa_p)],
                                axis=0).astype(BF16)
        att2 = _dot_nt(lhs_a, ka[:, sl])
        att.append((jnp.where(score_mask, att2[0:tb], 0.0).astype(BF16),
                    jnp.where(score_mask, att2[tb:2 * tb], 0.0).astype(BF16)))
    fill[0]()

    entering = []
    for p, sl in enumerate(pairs):
        kh_p = kh[:, sl]
        kh_routed = jnp.concatenate([jnp.where(in_chunk[c], kh_p, 0.0) for c in range(nc)],
                                    axis=1).astype(BF16)
        kv = _dot(vts[p], kh_routed)
        st = st_scr[p]
        ent = [None] * nc
        for c in order:
            ent[c] = st.astype(BF16)
            st = st * dec[c][:, sl] + jnp.where(own_lanes, kv[:, c * LANES:(c + 1) * LANES], 0.0)
        st_scr[p] = st
        entering.append(ent)
    fill[1]()

    for p, sl in enumerate(pairs):
        qs_p = qs[:, sl]
        qs_routed = jnp.concatenate([jnp.where(in_chunk[c], qs_p, 0.0) for c in range(nc)],
                                    axis=1).astype(BF16)
        vt_p = vts[p]
        vt_diag = jnp.concatenate(
            [jnp.concatenate([vt_p[0:DV_A], zero_vt], axis=1),
             jnp.concatenate([zero_vt, vt_p[DV_A:PAIR_V]], axis=1)], axis=0)
        w_nt = jnp.concatenate([vt_diag] + entering[p], axis=1)
        lhs_o = jnp.concatenate([att[p][0], att[p][1], qs_routed], axis=1)
        emit(p, _dot_nt(lhs_o, w_nt))


def _write_final_state(st_scr, stout_ref, slot):
    for h in range(H_A):
        hp = h % 2
        st_t = st_scr[h // 2, hp * DV_A:(hp + 1) * DV_A, :].T
        stout_ref[slot, h] = st_t[hp * DK_A:(hp + 1) * DK_A, :]


def _even_fwd_kernel(x_ref, mod_ref, ng_ref, wf_ref, wvt_ref, wlr_ref, wgk_ref, bgk_ref, sng_ref, ws_ref,
                     bs_ref, tri_ref, st0_ref,
                     h_ref, qk_ref, vt_ref, lr_ref, of_ref, sp_ref, stout_ref, st_scr, vs_scr, *, carry):
    t = pl.program_id(1)
    n_t = pl.num_programs(1)
    tb = x_ref.shape[0]
    n_sub = tb // GLA_BLOCK

    if carry:
        @pl.when(t == 0)
        def _():
            st_scr[...] = st0_ref[0]

    x = x_ref[...]
    shift = mod_ref[:, 0:D_MODEL]
    scale = mod_ref[:, D_MODEL:2 * D_MODEL]
    h = (_rms(x, ng_ref[...]) * (1.0 + scale) + shift).astype(BF16)
    h_ref[...] = h
    lr = _dot(h, wlr_ref[...])
    lr_ref[...] = lr
    g = _log_decay(lr.astype(BF16), wgk_ref, bgk_ref)
    qk = _dot(h, wf_ref[:, 0:512])
    q = qk[:, 0:256] * (DK_A ** -0.5)
    k = qk[:, 256:512]
    qk_ref[:, 0:256] = q
    qk_ref[:, 256:512] = k
    tri = tri_ref[...]
    b = [_cumsum_blocks(tri, g[s * GLA_BLOCK:(s + 1) * GLA_BLOCK]) for s in range(n_sub)]
    vt_ref[...] = _dot_nt(wvt_ref[...], h).astype(BF16)

    def sgu_input():
        vs = _dot(h, wf_ref[:, 512:1024])
        vs_scr[...] = _rms(vs, sng_ref[...]).astype(BF16)

    def sgu_mix():
        for n in range(tb // CHUNK_B):
            rs = slice(n * CHUNK_B, (n + 1) * CHUNK_B)
            for hb in range(H_B):
                cs = slice(hb * DH_B, (hb + 1) * DH_B)
                sp_ref[rs, cs] = (_dot(ws_ref[hb], vs_scr[rs, cs]) + bs_ref[hb]).astype(BF16)

    for s in range(n_sub):
        rows = slice(s * GLA_BLOCK, (s + 1) * GLA_BLOCK)
        if not carry:
            st_scr[...] = st0_ref[s]

        def emit(p, o, rows=rows):
            of_ref[rows, p * PAIR_V:(p + 1) * PAIR_V] = o

        fill = (sgu_input, sgu_mix) if s == 0 else (lambda: None, lambda: None)
        _gla_block(q[rows], k[rows], b[s], vt_ref, rows, st_scr, False, emit, fill)
        if not carry:
            _write_final_state(st_scr, stout_ref, s)

    if carry:
        @pl.when(t == n_t - 1)
        def _():
            _write_final_state(st_scr, stout_ref, 0)


def _even_bwd_kernel(x_ref, mod_ref, h_ref, qk_ref, vt_ref, lr_ref, of_ref, sp_ref, wb_ref, wgk_ref, bgk_ref,
                     gng_ref, wout_ref, tri_ref, st0_ref,
                     xo_ref, stout_ref, st_scr, mix_scr, outs_scr, *, carry):
    t = pl.program_id(1)
    n_t = pl.num_programs(1)
    tb = x_ref.shape[0]
    n_sub = tb // GLA_BLOCK

    if carry:
        @pl.when(t == 0)
        def _():
            st_scr[...] = st0_ref[0]

    h = h_ref[...]
    g = _log_decay(lr_ref[...].astype(BF16), wgk_ref, bgk_ref)
    ga = _dot(h, wb_ref[:, 0:512])
    tri = tri_ref[...]
    b = [_cumsum_blocks(tri, g[s * GLA_BLOCK:(s + 1) * GLA_BLOCK]) for s in range(n_sub)]

    def sgu_gate():
        ug = _dot(h, wb_ref[:, 512:1536])
        mix_scr[:, BRANCH_W:2 * BRANCH_W] = (ug[:, 0:512] * sp_ref[...].astype(F32)
                                             * _silu(ug[:, 512:1024])).astype(BF16)

    def sgu_out():
        outs_scr[...] = _dot(mix_scr[:, BRANCH_W:2 * BRANCH_W], wout_ref[BRANCH_W:2 * BRANCH_W, :])

    gng = gng_ref[...]
    order = range(n_sub - 1, -1, -1) if carry else range(n_sub)
    for i, s in enumerate(order):
        rows = slice(s * GLA_BLOCK, (s + 1) * GLA_BLOCK)
        if not carry:
            st_scr[...] = st0_ref[s]

        def emit(p, o_b, rows=rows):
            for hp in range(2):
                cs = slice(p * PAIR_V + hp * DV_A, p * PAIR_V + (hp + 1) * DV_A)
                o = of_ref[rows, cs] + o_b[:, hp * DV_A:(hp + 1) * DV_A]
                mix_scr[rows, cs] = (_rms(o, gng) * _silu(ga[rows, cs])).astype(BF16)

        fill = (sgu_gate, sgu_out) if i == 0 else (lambda: None, lambda: None)
        _gla_block(qk_ref[rows, 0:256], qk_ref[rows, 256:512], b[s], vt_ref, rows, st_scr, True, emit, fill)
        if not carry:
            _write_final_state(st_scr, stout_ref, s)

    out = _dot(mix_scr[:, 0:BRANCH_W], wout_ref[0:BRANCH_W, :]) + outs_scr[...]
    gate = mod_ref[:, 2 * D_MODEL:3 * D_MODEL]
    xo_ref[...] = x_ref[...] + gate * out

    if carry:
        @pl.when(t == n_t - 1)
        def _():
            _write_final_state(st_scr, stout_ref, 0)


def _window_sum(a, w):
    n = a.shape[0]
    ahead = lambda v, k: pltpu.roll(v, n - k, 0)
    behind = lambda v, k: pltpu.roll(v, k, 0)
    half = w // 2
    p, span = a, 1
    while span < half:
        p = p + ahead(p, span)
        span *= 2
    return (behind(p, half) + p)[HALO:n - HALO]


def _odd_kernel(xp_ref, x_ref, xn_ref, mod_ref, ng_ref, win_ref, wpool_ref, pscale_ref, wconv_ref,
                wout_ref, fng_ref, xo_ref, mix_scr, *, seq_len, n_sub, isolated, colmajor_out, final_norm):
    t = pl.program_id(1)
    n_t = pl.num_programs(1)
    tb = x_ref.shape[0]
    sb = tb // n_sub
    ne = sb + 2 * HALO

    shift = mod_ref[:, 0:D_MODEL]
    scale = mod_ref[:, D_MODEL:2 * D_MODEL]
    gate = mod_ref[:, 2 * D_MODEL:3 * D_MODEL]
    ng = ng_ref[...]
    if isolated:
        xe = x_ref[...]
        bounds = [s * sb for s in range(n_sub + 1)]
    else:
        xe = jnp.concatenate([xp_ref[...], x_ref[...], xn_ref[...]], axis=0)
        bounds = [0] + [ne + s * sb for s in range(n_sub)]

    zp = []
    for s in range(n_sub):
        h = (_rms(xe[bounds[s]:bounds[s + 1]], ng) * (1.0 + scale) + shift).astype(BF16)
        zp.append(_dot(h, win_ref[...]))

    mid = slice(HALO, HALO + sb)
    row_in_slab = lax.broadcasted_iota(jnp.int32, (sb, 1), 0)
    for s in range(n_sub):
        r0 = s * sb
        if isolated:
            zm = zp[s]
            edge = jnp.zeros((HALO, BRANCH_W), F32)
            xc = jnp.concatenate([edge, zm[:, 0:512], edge], axis=0)
            u = jnp.concatenate([edge, zm[:, 2048:2560] * zm[:, 1024:1536], edge], axis=0)
            pos = row_in_slab
            x_mid = xe[r0:r0 + sb]
        else:
            z = zp[0] if s == 0 else jnp.concatenate([zp[s - 1][-2 * HALO:], zp[s]], axis=0)
            rowe = r0 + lax.broadcasted_iota(jnp.int32, (ne, 1), 0)
            valid = jnp.logical_and(jnp.logical_or(rowe >= HALO, t > 0),
                                    jnp.logical_or(rowe < tb + HALO, t < n_t - 1))
            xc = jnp.where(valid, z[:, 0:512], 0.0)
            u = jnp.where(valid, z[:, 2048:2560] * z[:, 1024:1536], 0.0)
            zm = z[mid]
            pos = t * tb + r0 + row_in_slab
            x_mid = xe[HALO + r0:HALO + r0 + sb]

        pooled = []
        for i, w in enumerate(POOL_WINDOWS):
            cs = slice(i * DG_C, (i + 1) * DG_C)
            lo = jnp.clip(pos - w // 2, 0, seq_len)
            hi = jnp.clip(pos + (w - w // 2), 0, seq_len)
            inv_cnt = 1.0 / (hi - lo).astype(F32)
            pooled.append(_window_sum(xc[:, cs], w) * inv_cnt - xc[mid, cs])
        pc = _dot(jnp.concatenate(pooled, axis=1).astype(BF16), wpool_ref[...])
        mix_scr[r0:r0 + sb, 0:BRANCH_W] = (pc * pscale_ref[...] * _silu(zm[:, 512:1024])).astype(BF16)

        y = (pltpu.roll(u, 1, 0) * wconv_ref[0:1, :] + u * wconv_ref[1:2, :]
             + pltpu.roll(u, ne - 1, 0) * wconv_ref[2:3, :])
        mix_scr[r0:r0 + sb, BRANCH_W:2 * BRANCH_W] = (zm[:, 1536:2048] * y[mid]
                                                      * _silu(zm[:, 2560:3072])).astype(BF16)

        out = _dot(mix_scr[r0:r0 + sb, :], wout_ref[...])
        xn = x_mid + gate * out
        if final_norm:
            xn = _rms(xn, fng_ref[...])
        if colmajor_out:
            for j in range(sb // GRID_W):
                xo_ref[:, r0 // GRID_W + j, :] = xn[j * GRID_W:(j + 1) * GRID_W]
        else:
            xo_ref[r0:r0 + sb, :] = xn


def _ada_kernel(c_ref, w_ref, b_ref, o_ref):
    o_ref[...] = _dot(_silu(c_ref[...]).astype(BF16), w_ref[...].astype(BF16)) + b_ref[...]


def _params(n_axes=2):
    return pltpu.CompilerParams(dimension_semantics=("arbitrary",) * n_axes,
                                vmem_limit_bytes=VMEM_LIMIT)


def _ada_mod(conds, w_ada, b_ada):
    n_col = 2
    cw = 3 * D_MODEL // n_col
    return pl.pallas_call(
        _ada_kernel,
        grid=(DEPTH, n_col),
        in_specs=[pl.BlockSpec((MOD_ROWS, D_MODEL), lambda l, j: (0, 0)),
                  pl.BlockSpec((None, D_MODEL, cw), lambda l, j: (l, 0, j)),
                  pl.BlockSpec((None, 1, cw), lambda l, j: (l, 0, j))],
        out_specs=pl.BlockSpec((None, MOD_ROWS, cw), lambda l, j: (l, 0, j)),
        out_shape=jax.ShapeDtypeStruct((DEPTH, MOD_ROWS, 3 * D_MODEL), F32),
        compiler_params=_params(2),
        name="ada_mod",
    )(conds, w_ada, b_ada.reshape(DEPTH, 1, 3 * D_MODEL))


def _tri_blocks(tb, upper):
    r = np.arange(tb)
    same = (r[:, None] // CHUNK_A) == (r[None, :] // CHUNK_A)
    tri = (r[None, :] >= r[:, None]) if upper else (r[None, :] <= r[:, None])
    return jnp.asarray((same & tri).astype(np.float32), dtype=BF16)


def _const_spec(shape):
    nd = len(shape)
    return pl.BlockSpec(shape, lambda b, t: (0,) * nd)


def _even_layer(x, mod, mod_row, batch, seq, lw, st0_f, st0_b):
    tb = EVEN_TOKEN_BLOCK
    carry = seq >= tb
    spp = 1 if carry else tb // seq
    assert seq % tb == 0 if carry else (seq == GLA_BLOCK and batch % spp == 0)
    n_t = seq * spp // tb
    n_grp = batch // spp
    n_tok = batch * seq
    n_blk = n_grp * n_t
    tok = lambda width: pl.BlockSpec((tb, width), lambda b, t: (b * n_t + t, 0))
    tok_rev = lambda width: pl.BlockSpec((tb, width), lambda b, t: (b * n_t + (n_t - 1 - t), 0))
    vt = pl.BlockSpec((N_PAIR * PAIR_V, tb), lambda b, t: (b * n_t + t, 0))
    vt_rev = pl.BlockSpec((N_PAIR * PAIR_V, tb), lambda b, t: (b * n_t + (n_t - 1 - t), 0))
    mod_spec = pl.BlockSpec((None, 1, 3 * D_MODEL), lambda b, t: (mod_row(b * spp), 0, 0))
    st_in = pl.BlockSpec((spp, N_PAIR, PAIR_V, LANES), lambda b, t: (b, 0, 0, 0))
    st_out = pl.BlockSpec((spp, H_A, DK_A, DV_A), lambda b, t: (b, 0, 0, 0))
    st_shape = jax.ShapeDtypeStruct((batch, H_A, DK_A, DV_A), F32)
    st_scratch = pltpu.VMEM((N_PAIR, PAIR_V, LANES), F32)
    gb = GLA_BLOCK

    h, qk, v_t, lr, of, sp, s_f = pl.pallas_call(
        functools.partial(_even_fwd_kernel, carry=carry),
        grid=(n_grp, n_t),
        in_specs=[tok(D_MODEL), mod_spec, _const_spec((1, D_MODEL)),
                  _const_spec((D_MODEL, 1024)), _const_spec((BRANCH_W, D_MODEL)), _const_spec((D_MODEL, LANES)),
                  _const_spec((LANES, 256)), _const_spec((1, 256)), _const_spec((1, BRANCH_W)),
                  _const_spec((H_B, CHUNK_B, CHUNK_B)), _const_spec((H_B, CHUNK_B, LANES)),
                  _const_spec((gb, gb)), st_in],
        out_specs=[tok(D_MODEL), tok(512), vt, tok(LANES), tok(512), tok(512), st_out],
        out_shape=[jax.ShapeDtypeStruct((n_tok, D_MODEL), BF16), jax.ShapeDtypeStruct((n_tok, 512), F32),
                   jax.ShapeDtypeStruct((n_blk * N_PAIR * PAIR_V, tb), BF16),
                   jax.ShapeDtypeStruct((n_tok, LANES), F32),
                   jax.ShapeDtypeStruct((n_tok, 512), F32), jax.ShapeDtypeStruct((n_tok, 512), BF16),
                   st_shape],
        scratch_shapes=[st_scratch, pltpu.VMEM((tb, BRANCH_W), BF16)],
        compiler_params=_params(2),
        name="even_fwd",
    )(x, mod, lw["norm_g"], lw["w_f"], lw["w_vt"], lw["w_lr"], lw["w_gk_f"], lw["b_gk_f"], lw["sgu_norm_g"],
      lw["w_s"], lw["b_s"], lw["tri_lo"], st0_f)

    x_new, s_b = pl.pallas_call(
        functools.partial(_even_bwd_kernel, carry=carry),
        grid=(n_grp, n_t),
        in_specs=[tok_rev(D_MODEL), mod_spec, tok_rev(D_MODEL), tok_rev(512), vt_rev, tok_rev(LANES),
                  tok_rev(512), tok_rev(512), _const_spec((D_MODEL, 1536)), _const_spec((LANES, 256)),
                  _const_spec((1, 256)), _const_spec((1, DV_A)), _const_spec((D_MODEL, D_MODEL)),
                  _const_spec((gb, gb)), st_in],
        out_specs=[tok_rev(D_MODEL), st_out],
        out_shape=[jax.ShapeDtypeStruct((n_tok, D_MODEL), F32), st_shape],
        scratch_shapes=[st_scratch, pltpu.VMEM((tb, 2 * BRANCH_W), BF16), pltpu.VMEM((tb, D_MODEL), F32)],
        compiler_params=_params(2),
        name="even_bwd",
    )(x, mod, h, qk, v_t, lr, of, sp, lw["w_b"], lw["w_gk_b"], lw["b_gk_b"], lw["gla_norm_g"], lw["w_out"],
      lw["tri_up"], st0_b)
    return x_new, s_f, s_b


def _odd_layer(x, mod, mod_row, batch, seq, lw, colmajor_out, final_norm, final_norm_g):
    tb = ODD_TOKEN_BLOCK
    isolated = seq < tb
    spp = tb // seq if isolated else 1
    assert (seq == ODD_SLAB and batch % spp == 0 and not colmajor_out) if isolated else seq % tb == 0
    n_t = seq * spp // tb
    n_grp = batch // spp
    n_tok = batch * seq
    hb = tb // HALO
    n_halo_blocks = n_tok // HALO
    tok = pl.BlockSpec((tb, D_MODEL), lambda b, t: (b * n_t + t, 0))
    prev = pl.BlockSpec((HALO, D_MODEL), lambda b, t: (jnp.maximum((b * n_t + t) * hb - 1, 0), 0))
    nxt = pl.BlockSpec((HALO, D_MODEL),
                       lambda b, t: (jnp.minimum((b * n_t + t + 1) * hb, n_halo_blocks - 1), 0))
    mod_spec = pl.BlockSpec((None, 1, 3 * D_MODEL), lambda b, t: (mod_row(b * spp), 0, 0))
    if colmajor_out:
        rows_per_block = tb // GRID_W
        out_spec = pl.BlockSpec((None, GRID_W, rows_per_block, D_MODEL), lambda b, t: (b, 0, t, 0))
        out_shape = jax.ShapeDtypeStruct((batch, GRID_W, seq // GRID_W, D_MODEL), F32)
    else:
        out_spec = tok
        out_shape = jax.ShapeDtypeStruct((n_tok, D_MODEL), F32)
    kern = functools.partial(_odd_kernel, seq_len=seq, n_sub=tb // ODD_SLAB, isolated=isolated,
                             colmajor_out=colmajor_out, final_norm=final_norm)
    out = pl.pallas_call(
        kern,
        grid=(n_grp, n_t),
        in_specs=[prev, tok, nxt, mod_spec, _const_spec((1, D_MODEL)), _const_spec((D_MODEL, 3072)),
                  _const_spec((BRANCH_W, BRANCH_W)), _const_spec((1, BRANCH_W)), _const_spec((3, BRANCH_W)),
                  _const_spec((D_MODEL, D_MODEL)), _const_spec((1, D_MODEL))],
        out_specs=out_spec,
        out_shape=out_shape,
        scratch_shapes=[pltpu.VMEM((tb, D_MODEL), BF16)],
        compiler_params=_params(2),
        name="odd_layer",
    )(x, x, x, mod, lw["norm_g"], lw["w_in"], lw["w_pool"], lw["pool_scale"], lw["w_conv"], lw["w_out"],
      final_norm_g)
    return out.reshape(n_tok, D_MODEL)


def _state_to_kernel_layout(s):
    st = jnp.swapaxes(s, -1, -2)
    parts = []
    for h in range(H_A):
        pad = ((0, 0), (0, 0), (DK_A, 0)) if h % 2 else ((0, 0), (0, 0), (0, DK_A))
        parts.append(jnp.pad(st[:, h], pad))
    return jnp.stack(parts, axis=1).reshape(s.shape[0], N_PAIR, PAIR_V, LANES)


def kernel(x_prompt, x_sample, c, state_gla, c_ctx, w_ada, b_ada, norm_g, w_in_even, w_in_odd, w_out, w_gk,
           b_gk, gla_norm_g, sgu_norm_g, w_s, b_s, w_pool, pool_scale, w_conv, final_norm_g):
    bp, tp, _ = x_prompt.shape
    bs, ts, _ = x_sample.shape
    assert ts == GRID_W * GRID_W
    assert 1 + bs <= MOD_ROWS

    conds = jnp.concatenate([c_ctx[None], c, jnp.zeros((MOD_ROWS - 1 - bs, D_MODEL), F32)], axis=0)
    mod = _ada_mod(conds, w_ada, b_ada).reshape(DEPTH * MOD_ROWS, 1, 3 * D_MODEL)

    tri_lo = _tri_blocks(GLA_BLOCK, upper=False)
    tri_up = _tri_blocks(GLA_BLOCK, upper=True)
    fng = final_norm_g.reshape(1, D_MODEL)

    def even_weights(l):
        j = l // 2
        w = w_in_even[j]
        w_f = jnp.concatenate([w[:, 0:512], w[:, 2080:2592]], axis=1).astype(BF16)
        w_vt = w[:, 512:1024].T.astype(BF16)
        w_b = jnp.concatenate([w[:, 1024:1536], w[:, 1568:2080], w[:, 2592:3104]], axis=1).astype(BF16)
        w_lr = jnp.pad(w[:, 1536:1568], ((0, 0), (0, LANES - 2 * GATE_RANK))).astype(BF16)
        gk_f = jnp.pad(w_gk[j, 0], ((0, LANES - GATE_RANK), (0, 0))).astype(BF16)
        gk_b = jnp.pad(w_gk[j, 1], ((GATE_RANK, LANES - 2 * GATE_RANK), (0, 0))).astype(BF16)
        return dict(norm_g=norm_g[l].reshape(1, D_MODEL), w_f=w_f, w_vt=w_vt, w_b=w_b, w_lr=w_lr,
                    w_gk_f=gk_f, w_gk_b=gk_b,
                    b_gk_f=b_gk[j, 0].reshape(1, 256), b_gk_b=b_gk[j, 1].reshape(1, 256),
                    sgu_norm_g=sgu_norm_g[j].reshape(1, BRANCH_W), w_s=w_s[j].astype(BF16),
                    b_s=jnp.broadcast_to(b_s[j][:, :, None], (H_B, CHUNK_B, LANES)),
                    gla_norm_g=gla_norm_g[j].reshape(1, DV_A), w_out=w_out[l].astype(BF16),
                    tri_lo=tri_lo, tri_up=tri_up)

    def odd_weights(l):
        j = l // 2
        return dict(norm_g=norm_g[l].reshape(1, D_MODEL), w_in=w_in_odd[j].astype(BF16),
                    w_pool=jax.scipy.linalg.block_diag(*w_pool[j]).astype(BF16), pool_scale=pool_scale[j].reshape(1, BRANCH_W),
                    w_conv=w_conv[j], w_out=w_out[l].astype(BF16))

    weights = [even_weights(l) if l % 2 == 0 else odd_weights(l) for l in range(DEPTH)]

    def run_stream(x, batch, seq, mod_row_of, inits, latent):
        x = x.reshape(batch * seq, D_MODEL)
        finals = []
        for l in range(DEPTH):
            mod_row = functools.partial(mod_row_of, l)
            if l % 2 == 0:
                x, s_f, s_b = _even_layer(x, mod, mod_row, batch, seq, weights[l], *inits[l // 2])
                finals.append(jnp.stack([s_f, s_b], axis=1))
            else:
                last = l == DEPTH - 1
                x = _odd_layer(x, mod, mod_row, batch, seq, weights[l], colmajor_out=latent,
                               final_norm=last, final_norm_g=fng)
        return x.reshape(batch, seq, D_MODEL), finals

    zero_st = jnp.zeros((bp, N_PAIR, PAIR_V, LANES), F32)
    ctx_init = [(zero_st, zero_st)] * (DEPTH // 2)
    y_prompt, finals = run_stream(x_prompt, bp, tp, lambda l, b: l * MOD_ROWS, ctx_init, False)
    new_state = jnp.stack(finals, axis=1)

    lat_init = [(_state_to_kernel_layout(state_gla[:, j, 0]), _state_to_kernel_layout(state_gla[:, j, 1]))
                for j in range(DEPTH // 2)]
    y_sample, _ = run_stream(x_sample, bs, ts, lambda l, b: l * MOD_ROWS + 1 + b, lat_init, True)
    return (y_prompt, y_sample, new_state)
```

```python
import functools

import numpy as np
import jax
import jax.numpy as jnp
from jax import lax
from jax.experimental import pallas as pl
from jax.experimental.pallas import tpu as pltpu

F32 = jnp.float32
BF16 = jnp.bfloat16

D_MODEL = 1024
DEPTH = 4
GRID_W = 64
BRANCH_W = 512
H_A = 4
DK_A = 64
DV_A = 128
GATE_RANK = 16
GATE_NORMALIZER = 16.0
CHUNK_A = 64
H_B = 4
DH_B = 128
CHUNK_B = 128
POOL_WINDOWS = (2, 4, 8, 16)
G_C = 4
DG_C = 128
EPS = 1e-6

LANES = 128
GLA_BLOCK = 256
EVEN_TOKEN_BLOCK = 1024
ODD_TOKEN_BLOCK = 1024
ODD_SLAB = 256
HALO = 8
MOD_ROWS = 16
VMEM_LIMIT = 56 * 1024 * 1024
N_PAIR = H_A // 2
PAIR_V = 2 * DV_A


def _dot(a, b):
    return jnp.dot(a, b, preferred_element_type=F32)


def _dot_nt(a, b):
    return lax.dot_general(a, b, (((1,), (1,)), ((), ())), preferred_element_type=F32)


def _rms(x, g):
    ms = jnp.mean(x * x, axis=-1, keepdims=True)
    return x * lax.rsqrt(ms + EPS) * g


def _silu(x):
    return x * (1.0 / (1.0 + jnp.exp(-x)))


def _log_sigmoid(x):
    return jnp.minimum(x, 0.0) - jnp.log1p(jnp.exp(-jnp.abs(x)))


def _split2_bf16(x):
    x1 = x.astype(BF16)
    x2 = (x - x1.astype(F32)).astype(BF16)
    return jnp.concatenate([x1, x2], axis=1)


def _cumsum_blocks(tri, g):
    w = g.shape[1]
    bc = _dot(tri, _split2_bf16(g))
    return bc[:, 0:w] + bc[:, w:2 * w]


def _log_decay(lr_bf16, wgk_ref, bgk_ref):
    gl = _dot(lr_bf16, wgk_ref[...]) + bgk_ref[...]
    return _log_sigmoid(gl) * (1.0 / GATE_NORMALIZER)


def _gla_block(q, k, b, vt_ref, vt_cols, st_scr, reverse, emit, fill=(lambda: None, lambda: None)):
    tb = q.shape[0]
    nc = tb // CHUNK_A
    shift = CHUNK_A.bit_length() - 1

    def chunk_rows(idx):
        return jnp.concatenate(
            [jnp.broadcast_to(b[c * CHUNK_A + idx:c * CHUNK_A + idx + 1], (CHUNK_A, b.shape[1]))
             for c in range(nc)], axis=0)

    end_idx = 0 if reverse else CHUNK_A - 1
    b_end = chunk_rows(end_idx)
    b_mid = chunk_rows(CHUNK_A // 2)
    qs = q * jnp.exp(b)
    qa = q * jnp.exp(b - b_mid)
    ka = (k * jnp.exp(b_mid - b)).astype(BF16)
    kh = k * jnp.exp(b_end - b)
    dec = [jnp.exp(b[c * CHUNK_A + end_idx:c * CHUNK_A + end_idx + 1]) for c in range(nc)]

    zero_blk = jnp.zeros((CHUNK_A, LANES), BF16)

    def route(a):
        return jnp.concatenate(
            [jnp.concatenate([a[c * CHUNK_A:(c + 1) * CHUNK_A] if j == c else zero_blk for j in range(nc)], axis=1)
             for c in range(nc)], axis=0)

    r = lax.broadcasted_iota(jnp.int32, (tb, tb), 0)
    cc = lax.broadcasted_iota(jnp.int32, (tb, tb), 1)
    same_chunk = jnp.right_shift(r, shift) == jnp.right_shift(cc, shift)
    score_mask = jnp.logical_and(same_chunk, (cc >= r) if reverse else (cc <= r))
    lane = lax.broadcasted_iota(jnp.int32, (tb, LANES), 1)
    lane_h0 = lane < DK_A
    srow = lax.broadcasted_iota(jnp.int32, (PAIR_V, LANES), 0)
    slane = lax.broadcasted_iota(jnp.int32, (PAIR_V, LANES), 1)
    own_lanes = (srow < DV_A) == (slane < DK_A)
    zero_vt = jnp.zeros((DV_A, tb), BF16)
    order = range(nc - 1, -1, -1) if reverse else range(nc)

    pairs = [slice(p * LANES, (p + 1) * LANES) for p in range(N_PAIR)]
    vts = [vt_ref[p * PAIR_V:(p + 1) * PAIR_V, vt_cols] for p in range(N_PAIR)]

    att = []
    for sl in pairs:
        qa_p = qa[:, sl]
        lhs_a = jnp.concatenate([jnp.where(lane_h0, qa_p, 0.0), jnp.where(lane_h0, 0.0, qa_p)],
                                axis=0).astype(BF16)
        att2 = _dot_nt(lhs_a, ka[:, sl])
        att.append((jnp.where(score_mask, att2[0:tb], 0.0).astype(BF16),
                    jnp.where(score_mask, att2[tb:2 * tb], 0.0).astype(BF16)))
    fill[0]()

    entering = []
    for p, sl in enumerate(pairs):
        kh_routed = route(kh[:, sl].astype(BF16))
        kv = _dot(vts[p], kh_routed)
        st = st_scr[p]
        ent = [None] * nc
        for c in order:
            ent[c] = st.astype(BF16)
            st = st * dec[c][:, sl] + jnp.where(own_lanes, kv[:, c * LANES:(c + 1) * LANES], 0.0)
        st_scr[p] = st
        entering.append(ent)
    fill[1]()

    for p, sl in enumerate(pairs):
        qs_routed = route(qs[:, sl].astype(BF16))
        vt_p = vts[p]
        vt_diag = jnp.concatenate(
            [jnp.concatenate([vt_p[0:DV_A], zero_vt], axis=1),
             jnp.concatenate([zero_vt, vt_p[DV_A:PAIR_V]], axis=1)], axis=0)
        w_nt = jnp.concatenate([vt_diag] + entering[p], axis=1)
        lhs_o = jnp.concatenate([att[p][0], att[p][1], qs_routed], axis=1)
        emit(p, _dot_nt(lhs_o, w_nt))


def _write_final_state(st_scr, stout_ref, slot):
    for h in range(H_A):
        hp = h % 2
        st_t = st_scr[h // 2, hp * DV_A:(hp + 1) * DV_A, :].T
        stout_ref[slot, h] = st_t[hp * DK_A:(hp + 1) * DK_A, :]


def _even_fwd_kernel(x_ref, mod_ref, ng_ref, wqk_ref, wvs_ref, wvt_ref, wlr_ref, wgk_ref, bgk_ref, sng_ref, ws_ref,
                     bs_ref, tri_ref, st0_ref,
                     h_ref, qk_ref, vt_ref, lr_ref, of_ref, sp_ref, stout_ref, st_scr, vs_scr, *, carry, zero_state):
    t = pl.program_id(1)
    n_t = pl.num_programs(1)
    tb = x_ref.shape[0]
    n_sub = tb // GLA_BLOCK

    if carry:
        @pl.when(t == 0)
        def _():
            st_scr[...] = jnp.zeros(st_scr.shape, F32) if zero_state else st0_ref[0]

    x = x_ref[...]
    shift = mod_ref[:, 0:D_MODEL]
    scale = mod_ref[:, D_MODEL:2 * D_MODEL]
    h = (_rms(x, ng_ref[...]) * (1.0 + scale) + shift).astype(BF16)
    h_ref[...] = h
    lr = _dot(h, wlr_ref[...])
    lr_ref[...] = lr
    g = _log_decay(lr.astype(BF16), wgk_ref, bgk_ref)
    qk = _dot(h, wqk_ref[...])
    q = qk[:, 0:256] * (DK_A ** -0.5)
    k = qk[:, 256:512]
    qk_ref[:, 0:256] = q
    qk_ref[:, 256:512] = k
    tri = tri_ref[...]
    b = [_cumsum_blocks(tri, g[s * GLA_BLOCK:(s + 1) * GLA_BLOCK]) for s in range(n_sub)]
    vt_ref[...] = _dot_nt(wvt_ref[...], h).astype(BF16)

    def sgu_input():
        vs = _dot(h, wvs_ref[...])
        vs_scr[...] = _rms(vs, sng_ref[...]).astype(BF16)

    def sgu_mix():
        for n in range(tb // CHUNK_B):
            rs = slice(n * CHUNK_B, (n + 1) * CHUNK_B)
            for hb in range(H_B):
                cs = slice(hb * DH_B, (hb + 1) * DH_B)
                sp_ref[rs, cs] = (_dot(ws_ref[hb], vs_scr[rs, cs]) + bs_ref[hb]).astype(BF16)

    for s in range(n_sub):
        rows = slice(s * GLA_BLOCK, (s + 1) * GLA_BLOCK)
        if not carry:
            st_scr[...] = jnp.zeros(st_scr.shape, F32) if zero_state else st0_ref[s]

        def emit(p, o, rows=rows):
            of_ref[rows, p * PAIR_V:(p + 1) * PAIR_V] = o

        fill = (sgu_input, sgu_mix) if s == 0 else (lambda: None, lambda: None)
        _gla_block(q[rows], k[rows], b[s], vt_ref, rows, st_scr, False, emit, fill)
        if not carry:
            _write_final_state(st_scr, stout_ref, s)

    if carry:
        @pl.when(t == n_t - 1)
        def _():
            _write_final_state(st_scr, stout_ref, 0)


def _even_bwd_kernel(x_ref, mod_ref, h_ref, qk_ref, vt_ref, lr_ref, of_ref, sp_ref, wga_ref, wu_ref, wgb_ref,
                     wgk_ref, bgk_ref,
                     gng_ref, wout_ref, tri_ref, st0_ref,
                     xo_ref, stout_ref, st_scr, mix_scr, outs_scr, *, carry, zero_state):
    t = pl.program_id(1)
    n_t = pl.num_programs(1)
    tb = x_ref.shape[0]
    n_sub = tb // GLA_BLOCK

    if carry:
        @pl.when(t == 0)
        def _():
            st_scr[...] = jnp.zeros(st_scr.shape, F32) if zero_state else st0_ref[0]

    h = h_ref[...]
    g = _log_decay(lr_ref[...].astype(BF16), wgk_ref, bgk_ref)
    ga = _dot(h, wga_ref[...])
    tri = tri_ref[...]
    b = [_cumsum_blocks(tri, g[s * GLA_BLOCK:(s + 1) * GLA_BLOCK]) for s in range(n_sub)]

    def sgu_gate():
        u = _dot(h, wu_ref[...])
        gate_b = _dot(h, wgb_ref[...])
        mix_scr[:, BRANCH_W:2 * BRANCH_W] = (u * sp_ref[...].astype(F32) * _silu(gate_b)).astype(BF16)

    def sgu_out():
        outs_scr[...] = _dot(mix_scr[:, BRANCH_W:2 * BRANCH_W], wout_ref[BRANCH_W:2 * BRANCH_W, :])

    gng = gng_ref[...]
    order = range(n_sub - 1, -1, -1) if carry else range(n_sub)
    for i, s in enumerate(order):
        rows = slice(s * GLA_BLOCK, (s + 1) * GLA_BLOCK)
        if not carry:
            st_scr[...] = jnp.zeros(st_scr.shape, F32) if zero_state else st0_ref[s]

        def emit(p, o_b, rows=rows):
            for hp in range(2):
                cs = slice(p * PAIR_V + hp * DV_A, p * PAIR_V + (hp + 1) * DV_A)
                o = of_ref[rows, cs] + o_b[:, hp * DV_A:(hp + 1) * DV_A]
                mix_scr[rows, cs] = (_rms(o, gng) * _silu(ga[rows, cs])).astype(BF16)

        fill = (sgu_gate, sgu_out) if i == 0 else (lambda: None, lambda: None)
        _gla_block(qk_ref[rows, 0:256], qk_ref[rows, 256:512], b[s], vt_ref, rows, st_scr, True, emit, fill)
        if not carry:
            _write_final_state(st_scr, stout_ref, s)

    out = _dot(mix_scr[:, 0:BRANCH_W], wout_ref[0:BRANCH_W, :]) + outs_scr[...]
    gate = mod_ref[:, 2 * D_MODEL:3 * D_MODEL]
    xo_ref[...] = x_ref[...] + gate * out

    if carry:
        @pl.when(t == n_t - 1)
        def _():
            _write_final_state(st_scr, stout_ref, 0)


def _window_sum(a, w):
    n = a.shape[0]
    ahead = lambda v, k: pltpu.roll(v, n - k, 0)
    behind = lambda v, k: pltpu.roll(v, k, 0)
    half = w // 2
    p, span = a, 1
    while span < half:
        p = p + ahead(p, span)
        span *= 2
    return (behind(p, half) + p)[HALO:n - HALO]


def _odd_kernel(xp_ref, x_ref, xn_ref, mod_ref, ng_ref, win_ref, wpool_ref, pscale_ref, wconv_ref,
                wout_ref, fng_ref, xo_ref, mix_scr, *, seq_len, n_sub, isolated, colmajor_out, final_norm):
    t = pl.program_id(1)
    n_t = pl.num_programs(1)
    tb = x_ref.shape[0]
    sb = tb // n_sub
    ne = sb + 2 * HALO

    shift = mod_ref[:, 0:D_MODEL]
    scale = mod_ref[:, D_MODEL:2 * D_MODEL]
    gate = mod_ref[:, 2 * D_MODEL:3 * D_MODEL]
    ng = ng_ref[...]
    if isolated:
        xe = x_ref[...]
        bounds = [s * sb for s in range(n_sub + 1)]
    else:
        xe = jnp.concatenate([xp_ref[...], x_ref[...], xn_ref[...]], axis=0)
        bounds = [0] + [ne + s * sb for s in range(n_sub)]

    zp = []
    for s in range(n_sub):
        h = (_rms(xe[bounds[s]:bounds[s + 1]], ng) * (1.0 + scale) + shift).astype(BF16)
        zp.append(_dot(h, win_ref[...]))

    mid = slice(HALO, HALO + sb)
    row_in_slab = lax.broadcasted_iota(jnp.int32, (sb, 1), 0)
    for s in range(n_sub):
        r0 = s * sb
        if isolated:
            zm = zp[s]
            edge = jnp.zeros((HALO, BRANCH_W), F32)
            xc = jnp.concatenate([edge, zm[:, 0:512], edge], axis=0)
            u = jnp.concatenate([edge, zm[:, 2048:2560] * zm[:, 1024:1536], edge], axis=0)
            pos = row_in_slab
            x_mid = xe[r0:r0 + sb]
        else:
            z = zp[0] if s == 0 else jnp.concatenate([zp[s - 1][-2 * HALO:], zp[s]], axis=0)
            rowe = r0 + lax.broadcasted_iota(jnp.int32, (ne, 1), 0)
            valid = jnp.logical_and(jnp.logical_or(rowe >= HALO, t > 0),
                                    jnp.logical_or(rowe < tb + HALO, t < n_t - 1))
            xc = jnp.where(valid, z[:, 0:512], 0.0)
            u = jnp.where(valid, z[:, 2048:2560] * z[:, 1024:1536], 0.0)
            zm = z[mid]
            pos = t * tb + r0 + row_in_slab
            x_mid = xe[HALO + r0:HALO + r0 + sb]

        pooled = []
        for i, w in enumerate(POOL_WINDOWS):
            cs = slice(i * DG_C, (i + 1) * DG_C)
            lo = jnp.clip(pos - w // 2, 0, seq_len)
            hi = jnp.clip(pos + (w - w // 2), 0, seq_len)
            inv_cnt = 1.0 / (hi - lo).astype(F32)
            pooled.append(_window_sum(xc[:, cs], w) * inv_cnt - xc[mid, cs])
        pc = _dot(jnp.concatenate(pooled, axis=1).astype(BF16), wpool_ref[...])
        mix_scr[r0:r0 + sb, 0:BRANCH_W] = (pc * pscale_ref[...] * _silu(zm[:, 512:1024])).astype(BF16)

        y = (pltpu.roll(u, 1, 0) * wconv_ref[0:1, :] + u * wconv_ref[1:2, :]
             + pltpu.roll(u, ne - 1, 0) * wconv_ref[2:3, :])
        mix_scr[r0:r0 + sb, BRANCH_W:2 * BRANCH_W] = (zm[:, 1536:2048] * y[mid]
                                                      * _silu(zm[:, 2560:3072])).astype(BF16)

        out = _dot(mix_scr[r0:r0 + sb, :], wout_ref[...])
        xn = x_mid + gate * out
        if final_norm:
            xn = _rms(xn, fng_ref[...])
        if colmajor_out:
            for j in range(sb // GRID_W):
                xo_ref[:, r0 // GRID_W + j, :] = xn[j * GRID_W:(j + 1) * GRID_W]
        else:
            xo_ref[r0:r0 + sb, :] = xn


def _ada_kernel(c_ref, w_ref, b_ref, o_ref):
    o_ref[...] = _dot(_silu(c_ref[...]).astype(BF16), w_ref[...].astype(BF16)) + b_ref[...]


def _params(n_axes=2):
    return pltpu.CompilerParams(dimension_semantics=("arbitrary",) * n_axes,
                                vmem_limit_bytes=VMEM_LIMIT)


def _ada_mod(conds, w_ada, b_ada):
    n_col = 2
    cw = 3 * D_MODEL // n_col
    return pl.pallas_call(
        _ada_kernel,
        grid=(DEPTH, n_col),
        in_specs=[pl.BlockSpec((MOD_ROWS, D_MODEL), lambda l, j: (0, 0)),
                  pl.BlockSpec((None, D_MODEL, cw), lambda l, j: (l, 0, j)),
                  pl.BlockSpec((None, 1, cw), lambda l, j: (l, 0, j))],
        out_specs=pl.BlockSpec((None, MOD_ROWS, cw), lambda l, j: (l, 0, j)),
        out_shape=jax.ShapeDtypeStruct((DEPTH, MOD_ROWS, 3 * D_MODEL), F32),
        compiler_params=_params(2),
        name="ada_mod",
    )(conds, w_ada, b_ada.reshape(DEPTH, 1, 3 * D_MODEL))


def _tri_blocks(tb, upper):
    r = np.arange(tb)
    same = (r[:, None] // CHUNK_A) == (r[None, :] // CHUNK_A)
    tri = (r[None, :] >= r[:, None]) if upper else (r[None, :] <= r[:, None])
    return jnp.asarray((same & tri).astype(np.float32), dtype=BF16)


def _const_spec(shape):
    nd = len(shape)
    return pl.BlockSpec(shape, lambda b, t: (0,) * nd)


def _even_layer(x, mod, mod_row, batch, seq, lw, st0_f, st0_b):
    tb = EVEN_TOKEN_BLOCK
    carry = seq >= tb
    spp = 1 if carry else tb // seq
    assert seq % tb == 0 if carry else (seq == GLA_BLOCK and batch % spp == 0)
    n_t = seq * spp // tb
    n_grp = batch // spp
    n_tok = batch * seq
    n_blk = n_grp * n_t
    tok = lambda width: pl.BlockSpec((tb, width), lambda b, t: (b * n_t + t, 0))
    tok_rev = lambda width: pl.BlockSpec((tb, width), lambda b, t: (b * n_t + (n_t - 1 - t), 0))
    vt = pl.BlockSpec((N_PAIR * PAIR_V, tb), lambda b, t: (b * n_t + t, 0))
    vt_rev = pl.BlockSpec((N_PAIR * PAIR_V, tb), lambda b, t: (b * n_t + (n_t - 1 - t), 0))
    mod_spec = pl.BlockSpec((None, 1, 3 * D_MODEL), lambda b, t: (mod_row(b * spp), 0, 0))
    zero_state = st0_f is None
    if zero_state:
        st0_f = st0_b = jnp.zeros((1, 1), F32)
        st_in = _const_spec((1, 1))
    else:
        st_in = pl.BlockSpec((spp, N_PAIR, PAIR_V, LANES), lambda b, t: (b, 0, 0, 0))
    st_out = pl.BlockSpec((spp, H_A, DK_A, DV_A), lambda b, t: (b, 0, 0, 0))
    st_shape = jax.ShapeDtypeStruct((batch, H_A, DK_A, DV_A), F32)
    st_scratch = pltpu.VMEM((N_PAIR, PAIR_V, LANES), F32)
    gb = GLA_BLOCK

    h, qk, v_t, lr, of, sp, s_f = pl.pallas_call(
        functools.partial(_even_fwd_kernel, carry=carry, zero_state=zero_state),
        grid=(n_grp, n_t),
        in_specs=[tok(D_MODEL), mod_spec, _const_spec((1, D_MODEL)),
                  _const_spec((D_MODEL, 512)), _const_spec((D_MODEL, BRANCH_W)), _const_spec((BRANCH_W, D_MODEL)),
                  _const_spec((D_MODEL, LANES)),
                  _const_spec((LANES, 256)), _const_spec((1, 256)), _const_spec((1, BRANCH_W)),
                  _const_spec((H_B, CHUNK_B, CHUNK_B)), _const_spec((H_B, CHUNK_B, LANES)),
                  _const_spec((gb, gb)), st_in],
        out_specs=[tok(D_MODEL), tok(512), vt, tok(LANES), tok(512), tok(512), st_out],
        out_shape=[jax.ShapeDtypeStruct((n_tok, D_MODEL), BF16), jax.ShapeDtypeStruct((n_tok, 512), F32),
                   jax.ShapeDtypeStruct((n_blk * N_PAIR * PAIR_V, tb), BF16),
                   jax.ShapeDtypeStruct((n_tok, LANES), F32),
                   jax.ShapeDtypeStruct((n_tok, 512), F32), jax.ShapeDtypeStruct((n_tok, 512), BF16),
                   st_shape],
        scratch_shapes=[st_scratch, pltpu.VMEM((tb, BRANCH_W), BF16)],
        compiler_params=_params(2),
        name="even_fwd",
    )(x, mod, lw["norm_g"], lw["w_qk"], lw["w_vs"], lw["w_vt"], lw["w_lr"], lw["w_gk_f"], lw["b_gk_f"], lw["sgu_norm_g"],
      lw["w_s"], lw["b_s"], lw["tri_lo"], st0_f)

    x_new, s_b = pl.pallas_call(
        functools.partial(_even_bwd_kernel, carry=carry, zero_state=zero_state),
        grid=(n_grp, n_t),
        in_specs=[tok_rev(D_MODEL), mod_spec, tok_rev(D_MODEL), tok_rev(512), vt_rev, tok_rev(LANES),
                  tok_rev(512), tok_rev(512), _const_spec((D_MODEL, BRANCH_W)), _const_spec((D_MODEL, BRANCH_W)),
                  _const_spec((D_MODEL, BRANCH_W)), _const_spec((LANES, 256)),
                  _const_spec((1, 256)), _const_spec((1, DV_A)), _const_spec((D_MODEL, D_MODEL)),
                  _const_spec((gb, gb)), st_in],
        out_specs=[tok_rev(D_MODEL), st_out],
        out_shape=[jax.ShapeDtypeStruct((n_tok, D_MODEL), F32), st_shape],
        scratch_shapes=[st_scratch, pltpu.VMEM((tb, 2 * BRANCH_W), BF16), pltpu.VMEM((tb, D_MODEL), F32)],
        compiler_params=_params(2),
        name="even_bwd",
    )(x, mod, h, qk, v_t, lr, of, sp, lw["w_ga"], lw["w_u"], lw["w_gb"], lw["w_gk_b"], lw["b_gk_b"], lw["gla_norm_g"], lw["w_out"],
      lw["tri_up"], st0_b)
    return x_new, s_f, s_b


def _odd_layer(x, mod, mod_row, batch, seq, lw, colmajor_out, final_norm, final_norm_g):
    tb = ODD_TOKEN_BLOCK
    isolated = seq < tb
    spp = tb // seq if isolated else 1
    assert (seq == ODD_SLAB and batch % spp == 0 and not colmajor_out) if isolated else seq % tb == 0
    n_t = seq * spp // tb
    n_grp = batch // spp
    n_tok = batch * seq
    hb = tb // HALO
    n_halo_blocks = n_tok // HALO
    tok = pl.BlockSpec((tb, D_MODEL), lambda b, t: (b * n_t + t, 0))
    prev = pl.BlockSpec((HALO, D_MODEL), lambda b, t: (jnp.maximum((b * n_t + t) * hb - 1, 0), 0))
    nxt = pl.BlockSpec((HALO, D_MODEL),
                       lambda b, t: (jnp.minimum((b * n_t + t + 1) * hb, n_halo_blocks - 1), 0))
    mod_spec = pl.BlockSpec((None, 1, 3 * D_MODEL), lambda b, t: (mod_row(b * spp), 0, 0))
    if colmajor_out:
        rows_per_block = tb // GRID_W
        out_spec = pl.BlockSpec((None, GRID_W, rows_per_block, D_MODEL), lambda b, t: (b, 0, t, 0))
        out_shape = jax.ShapeDtypeStruct((batch, GRID_W, seq // GRID_W, D_MODEL), F32)
    else:
        out_spec = tok
        out_shape = jax.ShapeDtypeStruct((n_tok, D_MODEL), F32)
    kern = functools.partial(_odd_kernel, seq_len=seq, n_sub=tb // ODD_SLAB, isolated=isolated,
                             colmajor_out=colmajor_out, final_norm=final_norm)
    out = pl.pallas_call(
        kern,
        grid=(n_grp, n_t),
        in_specs=[prev, tok, nxt, mod_spec, _const_spec((1, D_MODEL)), _const_spec((D_MODEL, 3072)),
                  _const_spec((BRANCH_W, BRANCH_W)), _const_spec((1, BRANCH_W)), _const_spec((3, BRANCH_W)),
                  _const_spec((D_MODEL, D_MODEL)), _const_spec((1, D_MODEL))],
        out_specs=out_spec,
        out_shape=out_shape,
        scratch_shapes=[pltpu.VMEM((tb, D_MODEL), BF16)],
        compiler_params=_params(2),
        name="odd_layer",
    )(x, x, x, mod, lw["norm_g"], lw["w_in"], lw["w_pool"], lw["pool_scale"], lw["w_conv"], lw["w_out"],
      final_norm_g)
    return out.reshape(n_tok, D_MODEL)


def _state_to_kernel_layout(s):
    st = jnp.swapaxes(s, -1, -2)
    parts = []
    for h in range(H_A):
        pad = ((0, 0), (0, 0), (DK_A, 0)) if h % 2 else ((0, 0), (0, 0), (0, DK_A))
        parts.append(jnp.pad(st[:, h], pad))
    return jnp.stack(parts, axis=1).reshape(s.shape[0], N_PAIR, PAIR_V, LANES)


def kernel(x_prompt, x_sample, c, state_gla, c_ctx, w_ada, b_ada, norm_g, w_in_even, w_in_odd, w_out, w_gk,
           b_gk, gla_norm_g, sgu_norm_g, w_s, b_s, w_pool, pool_scale, w_conv, final_norm_g):
    bp, tp, _ = x_prompt.shape
    bs, ts, _ = x_sample.shape
    assert ts == GRID_W * GRID_W
    assert 1 + bs <= MOD_ROWS

    conds = jnp.concatenate([c_ctx[None], c, jnp.zeros((MOD_ROWS - 1 - bs, D_MODEL), F32)], axis=0)
    mod = _ada_mod(conds, w_ada, b_ada).reshape(DEPTH * MOD_ROWS, 1, 3 * D_MODEL)

    tri_lo = _tri_blocks(GLA_BLOCK, upper=False)
    tri_up = _tri_blocks(GLA_BLOCK, upper=True)
    fng = final_norm_g.reshape(1, D_MODEL)

    def even_weights(l):
        j = l // 2
        w = w_in_even[j]
        w_vt = w[:, 512:1024].T.astype(BF16)
        w_lr = jnp.pad(w[:, 1536:1568], ((0, 0), (0, LANES - 2 * GATE_RANK))).astype(BF16)
        gk_f = jnp.pad(w_gk[j, 0], ((0, LANES - GATE_RANK), (0, 0))).astype(BF16)
        gk_b = jnp.pad(w_gk[j, 1], ((GATE_RANK, LANES - 2 * GATE_RANK), (0, 0))).astype(BF16)
        return dict(norm_g=norm_g[l].reshape(1, D_MODEL), w_qk=w[:, 0:512].astype(BF16),
                    w_vs=w[:, 2080:2592].astype(BF16), w_vt=w_vt, w_ga=w[:, 1024:1536].astype(BF16),
                    w_u=w[:, 1568:2080].astype(BF16), w_gb=w[:, 2592:3104].astype(BF16), w_lr=w_lr,
                    w_gk_f=gk_f, w_gk_b=gk_b,
                    b_gk_f=b_gk[j, 0].reshape(1, 256), b_gk_b=b_gk[j, 1].reshape(1, 256),
                    sgu_norm_g=sgu_norm_g[j].reshape(1, BRANCH_W), w_s=w_s[j].astype(BF16),
                    b_s=jnp.broadcast_to(b_s[j][:, :, None], (H_B, CHUNK_B, LANES)),
                    gla_norm_g=gla_norm_g[j].reshape(1, DV_A), w_out=w_out[l].astype(BF16),
                    tri_lo=tri_lo, tri_up=tri_up)

    def odd_weights(l):
        j = l // 2
        return dict(norm_g=norm_g[l].reshape(1, D_MODEL), w_in=w_in_odd[j].astype(BF16),
                    w_pool=jax.scipy.linalg.block_diag(*w_pool[j]).astype(BF16), pool_scale=pool_scale[j].reshape(1, BRANCH_W),
                    w_conv=w_conv[j], w_out=w_out[l].astype(BF16))

    weights = [even_weights(l) if l % 2 == 0 else odd_weights(l) for l in range(DEPTH)]

    def run_stream(x, batch, seq, mod_row_of, inits, latent):
        x = x.reshape(batch * seq, D_MODEL)
        finals = []
        for l in range(DEPTH):
            mod_row = functools.partial(mod_row_of, l)
            if l % 2 == 0:
                x, s_f, s_b = _even_layer(x, mod, mod_row, batch, seq, weights[l], *inits[l // 2])
                finals += [s_f, s_b]
            else:
                last = l == DEPTH - 1
                x = _odd_layer(x, mod, mod_row, batch, seq, weights[l], colmajor_out=latent,
                               final_norm=last, final_norm_g=fng)
        return x.reshape(batch, seq, D_MODEL), finals

    ctx_init = [(None, None)] * (DEPTH // 2)
    y_prompt, finals = run_stream(x_prompt, bp, tp, lambda l, b: l * MOD_ROWS, ctx_init, False)
    new_state = jnp.stack(finals, axis=1).reshape(bp, DEPTH // 2, 2, H_A, DK_A, DV_A)

    lat_init = [(_state_to_kernel_layout(state_gla[:, j, 0]), _state_to_kernel_layout(state_gla[:, j, 1]))
                for j in range(DEPTH // 2)]
    y_sample, _ = run_stream(x_sample, bs, ts, lambda l, b: l * MOD_ROWS + 1 + b, lat_init, True)
    return (y_prompt, y_sample, new_state)
```

```python
import functools

import numpy as np
import jax
import jax.numpy as jnp
from jax import lax
from jax.experimental import pallas as pl
from jax.experimental.pallas import tpu as pltpu

F32 = jnp.float32
BF16 = jnp.bfloat16

D_MODEL = 1024
DEPTH = 4
GRID_W = 64
BRANCH_W = 512
H_A = 4
DK_A = 64
DV_A = 128
GATE_RANK = 16
GATE_NORMALIZER = 16.0
CHUNK_A = 64
H_B = 4
DH_B = 128
CHUNK_B = 128
POOL_WINDOWS = (2, 4, 8, 16)
G_C = 4
DG_C = 128
EPS = 1e-6

LANES = 128
GLA_BLOCK = 256
EVEN_TOKEN_BLOCK = 1024
ODD_TOKEN_BLOCK = 1024
ODD_SLAB = 256
HALO = 8
MOD_ROWS = 16
VMEM_LIMIT = 56 * 1024 * 1024
N_PAIR = H_A // 2
PAIR_V = 2 * DV_A


def _dot(a, b):
    return jnp.dot(a, b, preferred_element_type=F32)


def _dot_nt(a, b):
    return lax.dot_general(a, b, (((1,), (1,)), ((), ())), preferred_element_type=F32)


def _rms(x, g):
    ms = jnp.mean(x * x, axis=-1, keepdims=True)
    return x * lax.rsqrt(ms + EPS) * g


def _silu(x):
    return x * (1.0 / (1.0 + jnp.exp(-x)))


def _log_sigmoid(x):
    return jnp.minimum(x, 0.0) - jnp.log1p(jnp.exp(-jnp.abs(x)))


def _split2_bf16(x):
    x1 = x.astype(BF16)
    x2 = (x - x1.astype(F32)).astype(BF16)
    return jnp.concatenate([x1, x2], axis=1)


def _cumsum_blocks(tri, g):
    w = g.shape[1]
    bc = _dot(tri, _split2_bf16(g))
    return bc[:, 0:w] + bc[:, w:2 * w]


def _log_decay(lr_bf16, wgk_ref, bgk_ref):
    gl = _dot(lr_bf16, wgk_ref[...]) + bgk_ref[...]
    return _log_sigmoid(gl) * (1.0 / GATE_NORMALIZER)


def _gla_block(q, k, b, vt_ref, vt_cols, st_scr, reverse, emit, fill=(lambda: None, lambda: None)):
    tb = q.shape[0]
    nc = tb // CHUNK_A
    shift = CHUNK_A.bit_length() - 1

    def chunk_rows(idx):
        return jnp.concatenate(
            [jnp.broadcast_to(b[c * CHUNK_A + idx:c * CHUNK_A + idx + 1], (CHUNK_A, b.shape[1]))
             for c in range(nc)], axis=0)

    end_idx = 0 if reverse else CHUNK_A - 1
    b_end = chunk_rows(end_idx)
    b_mid = chunk_rows(CHUNK_A // 2)
    qs = q * jnp.exp(b)
    qa = q * jnp.exp(b - b_mid)
    ka = (k * jnp.exp(b_mid - b)).astype(BF16)
    kh = k * jnp.exp(b_end - b)
    dec = [jnp.exp(b[c * CHUNK_A + end_idx:c * CHUNK_A + end_idx + 1]) for c in range(nc)]

    zero_blk = jnp.zeros((CHUNK_A, LANES), BF16)

    def route(a):
        return jnp.concatenate(
            [jnp.concatenate([a[c * CHUNK_A:(c + 1) * CHUNK_A] if j == c else zero_blk for j in range(nc)], axis=1)
             for c in range(nc)], axis=0)

    r = lax.broadcasted_iota(jnp.int32, (tb, tb), 0)
    cc = lax.broadcasted_iota(jnp.int32, (tb, tb), 1)
    same_chunk = jnp.right_shift(r, shift) == jnp.right_shift(cc, shift)
    score_mask = jnp.logical_and(same_chunk, (cc >= r) if reverse else (cc <= r))
    lane = lax.broadcasted_iota(jnp.int32, (tb, LANES), 1)
    lane_h0 = lane < DK_A
    srow = lax.broadcasted_iota(jnp.int32, (PAIR_V, LANES), 0)
    slane = lax.broadcasted_iota(jnp.int32, (PAIR_V, LANES), 1)
    own_lanes = (srow < DV_A) == (slane < DK_A)
    zero_vt = jnp.zeros((DV_A, tb), BF16)
    order = range(nc - 1, -1, -1) if reverse else range(nc)

    pairs = [slice(p * LANES, (p + 1) * LANES) for p in range(N_PAIR)]
    vts = [vt_ref[p * PAIR_V:(p + 1) * PAIR_V, vt_cols] for p in range(N_PAIR)]

    att = []
    for sl in pairs:
        qa_p = qa[:, sl]
        lhs_a = jnp.concatenate([jnp.where(lane_h0, qa_p, 0.0), jnp.where(lane_h0, 0.0, qa_p)],
                                axis=0).astype(BF16)
        att2 = _dot_nt(lhs_a, ka[:, sl])
        att.append((jnp.where(score_mask, att2[0:tb], 0.0).astype(BF16),
                    jnp.where(score_mask, att2[tb:2 * tb], 0.0).astype(BF16)))
    fill[0]()

    entering = []
    for p, sl in enumerate(pairs):
        kh_routed = route(kh[:, sl].astype(BF16))
        kv = _dot(vts[p], kh_routed)
        st = st_scr[p]
        ent = [None] * nc
        for c in order:
            ent[c] = st.astype(BF16)
            st = st * dec[c][:, sl] + jnp.where(own_lanes, kv[:, c * LANES:(c + 1) * LANES], 0.0)
        st_scr[p] = st
        entering.append(ent)
    fill[1]()

    for p, sl in enumerate(pairs):
        qs_routed = route(qs[:, sl].astype(BF16))
        vt_p = vts[p]
        vt_diag = jnp.concatenate(
            [jnp.concatenate([vt_p[0:DV_A], zero_vt], axis=1),
             jnp.concatenate([zero_vt, vt_p[DV_A:PAIR_V]], axis=1)], axis=0)
        w_nt = jnp.concatenate([vt_diag] + entering[p], axis=1)
        lhs_o = jnp.concatenate([att[p][0], att[p][1], qs_routed], axis=1)
        emit(p, _dot_nt(lhs_o, w_nt))


def _load_state(st_scr, st0_ref, slot):
    for h in range(H_A):
        hp = h % 2
        zero = jnp.zeros((DK_A, DV_A), F32)
        s = st0_ref[slot, h]
        padded = jnp.concatenate([s, zero] if hp == 0 else [zero, s], axis=0)
        st_scr[h // 2, hp * DV_A:(hp + 1) * DV_A, :] = padded.T


def _write_final_state(st_scr, stout_ref, slot):
    for h in range(H_A):
        hp = h % 2
        st_t = st_scr[h // 2, hp * DV_A:(hp + 1) * DV_A, :].T
        stout_ref[slot, h] = st_t[hp * DK_A:(hp + 1) * DK_A, :]


def _even_fwd_kernel(x_ref, mod_ref, ng_ref, wqk_ref, wvs_ref, wvt_ref, wlr_ref, wgk_ref, bgk_ref, sng_ref, ws_ref,
                     bs_ref, tri_ref, st0_ref,
                     h_ref, qk_ref, vt_ref, lr_ref, of_ref, sp_ref, stout_ref, st_scr, vs_scr, *, carry, zero_state):
    t = pl.program_id(1)
    n_t = pl.num_programs(1)
    tb = x_ref.shape[0]
    n_sub = tb // GLA_BLOCK

    if carry:
        @pl.when(t == 0)
        def _():
            if zero_state:
                st_scr[...] = jnp.zeros(st_scr.shape, F32)
            else:
                _load_state(st_scr, st0_ref, 0)

    x = x_ref[...]
    shift = mod_ref[:, 0:D_MODEL]
    scale = mod_ref[:, D_MODEL:2 * D_MODEL]
    h = (_rms(x, ng_ref[...]) * (1.0 + scale) + shift).astype(BF16)
    h_ref[...] = h
    lr = _dot(h, wlr_ref[...])
    lr_ref[...] = lr
    g = _log_decay(lr.astype(BF16), wgk_ref, bgk_ref)
    qk = _dot(h, wqk_ref[...])
    q = qk[:, 0:256] * (DK_A ** -0.5)
    k = qk[:, 256:512]
    qk_ref[:, 0:256] = q
    qk_ref[:, 256:512] = k
    tri = tri_ref[...]
    b = [_cumsum_blocks(tri, g[s * GLA_BLOCK:(s + 1) * GLA_BLOCK]) for s in range(n_sub)]
    vt_ref[...] = _dot_nt(wvt_ref[...], h).astype(BF16)

    def sgu_input():
        vs = _dot(h, wvs_ref[...])
        vs_scr[...] = _rms(vs, sng_ref[...]).astype(BF16)

    def sgu_mix():
        for n in range(tb // CHUNK_B):
            rs = slice(n * CHUNK_B, (n + 1) * CHUNK_B)
            for hb in range(H_B):
                cs = slice(hb * DH_B, (hb + 1) * DH_B)
                sp_ref[rs, cs] = (_dot(ws_ref[hb], vs_scr[rs, cs]) + bs_ref[hb]).astype(BF16)

    for s in range(n_sub):
        rows = slice(s * GLA_BLOCK, (s + 1) * GLA_BLOCK)
        if not carry:
            if zero_state:
                st_scr[...] = jnp.zeros(st_scr.shape, F32)
            else:
                _load_state(st_scr, st0_ref, s)

        def emit(p, o, rows=rows):
            of_ref[rows, p * PAIR_V:(p + 1) * PAIR_V] = o

        fill = (sgu_input, sgu_mix) if s == 0 else (lambda: None, lambda: None)
        _gla_block(q[rows], k[rows], b[s], vt_ref, rows, st_scr, False, emit, fill)
        if not carry:
            _write_final_state(st_scr, stout_ref, s)

    if carry:
        @pl.when(t == n_t - 1)
        def _():
            _write_final_state(st_scr, stout_ref, 0)


def _even_bwd_kernel(x_ref, mod_ref, h_ref, qk_ref, vt_ref, lr_ref, of_ref, sp_ref, wga_ref, wu_ref, wgb_ref,
                     wgk_ref, bgk_ref,
                     gng_ref, wout_ref, tri_ref, st0_ref,
                     xo_ref, stout_ref, st_scr, mix_scr, outs_scr, *, carry, zero_state):
    t = pl.program_id(1)
    n_t = pl.num_programs(1)
    tb = x_ref.shape[0]
    n_sub = tb // GLA_BLOCK

    if carry:
        @pl.when(t == 0)
        def _():
            if zero_state:
                st_scr[...] = jnp.zeros(st_scr.shape, F32)
            else:
                _load_state(st_scr, st0_ref, 0)

    h = h_ref[...]
    g = _log_decay(lr_ref[...].astype(BF16), wgk_ref, bgk_ref)
    ga = _dot(h, wga_ref[...])
    tri = tri_ref[...]
    b = [_cumsum_blocks(tri, g[s * GLA_BLOCK:(s + 1) * GLA_BLOCK]) for s in range(n_sub)]

    def sgu_gate():
        u = _dot(h, wu_ref[...])
        gate_b = _dot(h, wgb_ref[...])
        mix_scr[:, BRANCH_W:2 * BRANCH_W] = (u * sp_ref[...].astype(F32) * _silu(gate_b)).astype(BF16)

    def sgu_out():
        outs_scr[...] = _dot(mix_scr[:, BRANCH_W:2 * BRANCH_W], wout_ref[BRANCH_W:2 * BRANCH_W, :])

    gng = gng_ref[...]
    order = range(n_sub - 1, -1, -1) if carry else range(n_sub)
    for i, s in enumerate(order):
        rows = slice(s * GLA_BLOCK, (s + 1) * GLA_BLOCK)
        if not carry:
            if zero_state:
                st_scr[...] = jnp.zeros(st_scr.shape, F32)
            else:
                _load_state(st_scr, st0_ref, s)

        def emit(p, o_b, rows=rows):
            for hp in range(2):
                cs = slice(p * PAIR_V + hp * DV_A, p * PAIR_V + (hp + 1) * DV_A)
                o = of_ref[rows, cs] + o_b[:, hp * DV_A:(hp + 1) * DV_A]
                mix_scr[rows, cs] = (_rms(o, gng) * _silu(ga[rows, cs])).astype(BF16)

        fill = (sgu_gate, sgu_out) if i == 0 else (lambda: None, lambda: None)
        _gla_block(qk_ref[rows, 0:256], qk_ref[rows, 256:512], b[s], vt_ref, rows, st_scr, True, emit, fill)
        if not carry:
            _write_final_state(st_scr, stout_ref, s)

    out = _dot(mix_scr[:, 0:BRANCH_W], wout_ref[0:BRANCH_W, :]) + outs_scr[...]
    gate = mod_ref[:, 2 * D_MODEL:3 * D_MODEL]
    xo_ref[...] = x_ref[...] + gate * out

    if carry:
        @pl.when(t == n_t - 1)
        def _():
            _write_final_state(st_scr, stout_ref, 0)


def _window_sum(a, w):
    n = a.shape[0]
    ahead = lambda v, k: pltpu.roll(v, n - k, 0)
    behind = lambda v, k: pltpu.roll(v, k, 0)
    half = w // 2
    p, span = a, 1
    while span < half:
        p = p + ahead(p, span)
        span *= 2
    return (behind(p, half) + p)[HALO:n - HALO]


def _odd_kernel(xp_ref, x_ref, xn_ref, mod_ref, ng_ref, win_ref, wpool_ref, pscale_ref, wconv_ref,
                wout_ref, fng_ref, xo_ref, mix_scr, *, seq_len, n_sub, isolated, colmajor_out, final_norm):
    t = pl.program_id(1)
    n_t = pl.num_programs(1)
    tb = x_ref.shape[0]
    sb = tb // n_sub
    ne = sb + 2 * HALO

    shift = mod_ref[:, 0:D_MODEL]
    scale = mod_ref[:, D_MODEL:2 * D_MODEL]
    gate = mod_ref[:, 2 * D_MODEL:3 * D_MODEL]
    ng = ng_ref[...]
    if isolated:
        xe = x_ref[...]
        bounds = [s * sb for s in range(n_sub + 1)]
    else:
        xe = jnp.concatenate([xp_ref[...], x_ref[...], xn_ref[...]], axis=0)
        bounds = [0] + [ne + s * sb for s in range(n_sub)]

    zp = []
    for s in range(n_sub):
        h = (_rms(xe[bounds[s]:bounds[s + 1]], ng) * (1.0 + scale) + shift).astype(BF16)
        zp.append(_dot(h, win_ref[...]))

    mid = slice(HALO, HALO + sb)
    row_in_slab = lax.broadcasted_iota(jnp.int32, (sb, 1), 0)
    for s in range(n_sub):
        r0 = s * sb
        if isolated:
            zm = zp[s]
            edge = jnp.zeros((HALO, BRANCH_W), F32)
            xc = jnp.concatenate([edge, zm[:, 0:512], edge], axis=0)
            u = jnp.concatenate([edge, zm[:, 2048:2560] * zm[:, 1024:1536], edge], axis=0)
            pos = row_in_slab
            x_mid = xe[r0:r0 + sb]
        else:
            z = zp[0] if s == 0 else jnp.concatenate([zp[s - 1][-2 * HALO:], zp[s]], axis=0)
            rowe = r0 + lax.broadcasted_iota(jnp.int32, (ne, 1), 0)
            valid = jnp.logical_and(jnp.logical_or(rowe >= HALO, t > 0),
                                    jnp.logical_or(rowe < tb + HALO, t < n_t - 1))
            xc = jnp.where(valid, z[:, 0:512], 0.0)
            u = jnp.where(valid, z[:, 2048:2560] * z[:, 1024:1536], 0.0)
            zm = z[mid]
            pos = t * tb + r0 + row_in_slab
            x_mid = xe[HALO + r0:HALO + r0 + sb]

        pooled = []
        for i, w in enumerate(POOL_WINDOWS):
            cs = slice(i * DG_C, (i + 1) * DG_C)
            lo = jnp.clip(pos - w // 2, 0, seq_len)
            hi = jnp.clip(pos + (w - w // 2), 0, seq_len)
            inv_cnt = 1.0 / (hi - lo).astype(F32)
            pooled.append(_window_sum(xc[:, cs], w) * inv_cnt - xc[mid, cs])
        pc = _dot(jnp.concatenate(pooled, axis=1).astype(BF16), wpool_ref[...])
        mix_scr[r0:r0 + sb, 0:BRANCH_W] = (pc * pscale_ref[...] * _silu(zm[:, 512:1024])).astype(BF16)

        y = (pltpu.roll(u, 1, 0) * wconv_ref[0:1, :] + u * wconv_ref[1:2, :]
             + pltpu.roll(u, ne - 1, 0) * wconv_ref[2:3, :])
        mix_scr[r0:r0 + sb, BRANCH_W:2 * BRANCH_W] = (zm[:, 1536:2048] * y[mid]
                                                      * _silu(zm[:, 2560:3072])).astype(BF16)

        out = _dot(mix_scr[r0:r0 + sb, :], wout_ref[...])
        xn = x_mid + gate * out
        if final_norm:
            xn = _rms(xn, fng_ref[...])
        if colmajor_out:
            for j in range(sb // GRID_W):
                xo_ref[:, r0 // GRID_W + j, :] = xn[j * GRID_W:(j + 1) * GRID_W]
        else:
            xo_ref[r0:r0 + sb, :] = xn


def _ada_kernel(c_ref, w_ref, b_ref, o_ref):
    o_ref[...] = _dot(_silu(c_ref[...]).astype(BF16), w_ref[...].astype(BF16)) + b_ref[...]


def _params(n_axes=2):
    return pltpu.CompilerParams(dimension_semantics=("arbitrary",) * n_axes,
                                vmem_limit_bytes=VMEM_LIMIT)


def _ada_mod(conds, w_ada, b_ada):
    n_col = 2
    cw = 3 * D_MODEL // n_col
    return pl.pallas_call(
        _ada_kernel,
        grid=(DEPTH, n_col),
        in_specs=[pl.BlockSpec((MOD_ROWS, D_MODEL), lambda l, j: (0, 0)),
                  pl.BlockSpec((None, D_MODEL, cw), lambda l, j: (l, 0, j)),
                  pl.BlockSpec((None, 1, cw), lambda l, j: (l, 0, j))],
        out_specs=pl.BlockSpec((None, MOD_ROWS, cw), lambda l, j: (l, 0, j)),
        out_shape=jax.ShapeDtypeStruct((DEPTH, MOD_ROWS, 3 * D_MODEL), F32),
        compiler_params=_params(2),
        name="ada_mod",
    )(conds, w_ada, b_ada.reshape(DEPTH, 1, 3 * D_MODEL))


def _tri_blocks(tb, upper):
    r = np.arange(tb)
    same = (r[:, None] // CHUNK_A) == (r[None, :] // CHUNK_A)
    tri = (r[None, :] >= r[:, None]) if upper else (r[None, :] <= r[:, None])
    return jnp.asarray((same & tri).astype(np.float32), dtype=BF16)


def _const_spec(shape):
    nd = len(shape)
    return pl.BlockSpec(shape, lambda b, t: (0,) * nd)


def _even_layer(x, mod, mod_row, batch, seq, lw, init_states):
    tb = EVEN_TOKEN_BLOCK
    carry = seq >= tb
    spp = 1 if carry else tb // seq
    assert seq % tb == 0 if carry else (seq == GLA_BLOCK and batch % spp == 0)
    n_t = seq * spp // tb
    n_grp = batch // spp
    n_tok = batch * seq
    n_blk = n_grp * n_t
    tok = lambda width: pl.BlockSpec((tb, width), lambda b, t: (b * n_t + t, 0))
    tok_rev = lambda width: pl.BlockSpec((tb, width), lambda b, t: (b * n_t + (n_t - 1 - t), 0))
    vt = pl.BlockSpec((N_PAIR * PAIR_V, tb), lambda b, t: (b * n_t + t, 0))
    vt_rev = pl.BlockSpec((N_PAIR * PAIR_V, tb), lambda b, t: (b * n_t + (n_t - 1 - t), 0))
    mod_spec = pl.BlockSpec((None, 1, 3 * D_MODEL), lambda b, t: (mod_row(b * spp), 0, 0))
    zero_state = init_states is None
    j = lw["pair"]
    if zero_state:
        init_states = jnp.zeros((1, 1), F32)
        st_in = [_const_spec((1, 1))] * 2
    else:
        st_in = [pl.BlockSpec((spp, None, None, H_A, DK_A, DV_A), lambda b, t, d=d: (b, j, d, 0, 0, 0))
                 for d in range(2)]
    w_cols = lambda width, blk: pl.BlockSpec((None, D_MODEL, width), lambda b, t: (j, 0, blk))
    w_out_spec = pl.BlockSpec((None, D_MODEL, D_MODEL), lambda b, t: (lw["layer"], 0, 0))
    st_out = pl.BlockSpec((spp, H_A, DK_A, DV_A), lambda b, t: (b, 0, 0, 0))
    st_shape = jax.ShapeDtypeStruct((batch, H_A, DK_A, DV_A), F32)
    st_scratch = pltpu.VMEM((N_PAIR, PAIR_V, LANES), F32)
    gb = GLA_BLOCK

    h, qk, v_t, lr, of, sp, s_f = pl.pallas_call(
        functools.partial(_even_fwd_kernel, carry=carry, zero_state=zero_state),
        grid=(n_grp, n_t),
        in_specs=[tok(D_MODEL), mod_spec, _const_spec((1, D_MODEL)),
                  w_cols(512, 0), w_cols(BRANCH_W, 4), _const_spec((BRANCH_W, D_MODEL)),
                  w_cols(LANES, 3072 // LANES),
                  _const_spec((LANES, 256)), _const_spec((1, 256)), _const_spec((1, BRANCH_W)),
                  _const_spec((H_B, CHUNK_B, CHUNK_B)), _const_spec((H_B, CHUNK_B, LANES)),
                  _const_spec((gb, gb)), st_in[0]],
        out_specs=[tok(D_MODEL), tok(512), vt, tok(LANES), tok(512), tok(512), st_out],
        out_shape=[jax.ShapeDtypeStruct((n_tok, D_MODEL), BF16), jax.ShapeDtypeStruct((n_tok, 512), F32),
                   jax.ShapeDtypeStruct((n_blk * N_PAIR * PAIR_V, tb), BF16),
                   jax.ShapeDtypeStruct((n_tok, LANES), F32),
                   jax.ShapeDtypeStruct((n_tok, 512), F32), jax.ShapeDtypeStruct((n_tok, 512), BF16),
                   st_shape],
        scratch_shapes=[st_scratch, pltpu.VMEM((tb, BRANCH_W), BF16)],
        compiler_params=_params(2),
        name="even_fwd",
    )(x, mod, lw["norm_g"], lw["w_in"], lw["w_in"], lw["w_vt"], lw["w_in"], lw["w_gk_f"], lw["b_gk_f"],
      lw["sgu_norm_g"], lw["w_s"], lw["b_s"], lw["tri_lo"], init_states)

    x_new, s_b = pl.pallas_call(
        functools.partial(_even_bwd_kernel, carry=carry, zero_state=zero_state),
        grid=(n_grp, n_t),
        in_specs=[tok_rev(D_MODEL), mod_spec, tok_rev(D_MODEL), tok_rev(512), vt_rev, tok_rev(LANES),
                  tok_rev(512), tok_rev(512), w_cols(BRANCH_W, 2), w_cols(BRANCH_W, 3), w_cols(BRANCH_W, 5),
                  _const_spec((LANES, 256)),
                  _const_spec((1, 256)), _const_spec((1, DV_A)), w_out_spec,
                  _const_spec((gb, gb)), st_in[1]],
        out_specs=[tok_rev(D_MODEL), st_out],
        out_shape=[jax.ShapeDtypeStruct((n_tok, D_MODEL), F32), st_shape],
        scratch_shapes=[st_scratch, pltpu.VMEM((tb, 2 * BRANCH_W), BF16), pltpu.VMEM((tb, D_MODEL), F32)],
        compiler_params=_params(2),
        name="even_bwd",
    )(x, mod, h, qk, v_t, lr, of, sp, lw["w_in"], lw["w_in"], lw["w_in"], lw["w_gk_b"], lw["b_gk_b"],
      lw["gla_norm_g"], lw["w_out"], lw["tri_up"], init_states)
    return x_new, s_f, s_b


def _odd_layer(x, mod, mod_row, batch, seq, lw, colmajor_out, final_norm, final_norm_g):
    tb = ODD_TOKEN_BLOCK
    isolated = seq < tb
    spp = tb // seq if isolated else 1
    assert (seq == ODD_SLAB and batch % spp == 0 and not colmajor_out) if isolated else seq % tb == 0
    n_t = seq * spp // tb
    n_grp = batch // spp
    n_tok = batch * seq
    hb = tb // HALO
    n_halo_blocks = n_tok // HALO
    tok = pl.BlockSpec((tb, D_MODEL), lambda b, t: (b * n_t + t, 0))
    prev = pl.BlockSpec((HALO, D_MODEL), lambda b, t: (jnp.maximum((b * n_t + t) * hb - 1, 0), 0))
    nxt = pl.BlockSpec((HALO, D_MODEL),
                       lambda b, t: (jnp.minimum((b * n_t + t + 1) * hb, n_halo_blocks - 1), 0))
    mod_spec = pl.BlockSpec((None, 1, 3 * D_MODEL), lambda b, t: (mod_row(b * spp), 0, 0))
    if colmajor_out:
        rows_per_block = tb // GRID_W
        out_spec = pl.BlockSpec((None, GRID_W, rows_per_block, D_MODEL), lambda b, t: (b, 0, t, 0))
        out_shape = jax.ShapeDtypeStruct((batch, GRID_W, seq // GRID_W, D_MODEL), F32)
    else:
        out_spec = tok
        out_shape = jax.ShapeDtypeStruct((n_tok, D_MODEL), F32)
    kern = functools.partial(_odd_kernel, seq_len=seq, n_sub=tb // ODD_SLAB, isolated=isolated,
                             colmajor_out=colmajor_out, final_norm=final_norm)
    out = pl.pallas_call(
        kern,
        grid=(n_grp, n_t),
        in_specs=[prev, tok, nxt, mod_spec, _const_spec((1, D_MODEL)),
                  pl.BlockSpec((None, D_MODEL, 3072), lambda b, t: (lw["pair"], 0, 0)),
                  _const_spec((BRANCH_W, BRANCH_W)), _const_spec((1, BRANCH_W)), _const_spec((3, BRANCH_W)),
                  pl.BlockSpec((None, D_MODEL, D_MODEL), lambda b, t: (lw["layer"], 0, 0)),
                  _const_spec((1, D_MODEL))],
        out_specs=out_spec,
        out_shape=out_shape,
        scratch_shapes=[pltpu.VMEM((tb, D_MODEL), BF16)],
        compiler_params=_params(2),
        name="odd_layer",
    )(x, x, x, mod, lw["norm_g"], lw["w_in"], lw["w_pool"], lw["pool_scale"], lw["w_conv"], lw["w_out"],
      final_norm_g)
    return out.reshape(n_tok, D_MODEL)


def kernel(x_prompt, x_sample, c, state_gla, c_ctx, w_ada, b_ada, norm_g, w_in_even, w_in_odd, w_out, w_gk,
           b_gk, gla_norm_g, sgu_norm_g, w_s, b_s, w_pool, pool_scale, w_conv, final_norm_g):
    bp, tp, _ = x_prompt.shape
    bs, ts, _ = x_sample.shape
    assert ts == GRID_W * GRID_W
    assert 1 + bs <= MOD_ROWS

    conds = jnp.concatenate([c_ctx[None], c, jnp.zeros((MOD_ROWS - 1 - bs, D_MODEL), F32)], axis=0)
    mod = _ada_mod(conds, w_ada, b_ada).reshape(DEPTH * MOD_ROWS, 1, 3 * D_MODEL)

    tri_lo = _tri_blocks(GLA_BLOCK, upper=False)
    tri_up = _tri_blocks(GLA_BLOCK, upper=True)
    fng = final_norm_g.reshape(1, D_MODEL)

    n_even = w_in_even.shape[0]
    w_even = jnp.concatenate(
        [w_in_even[:, :, 0:1536], w_in_even[:, :, 1568:3104], w_in_even[:, :, 1536:1568],
         jnp.zeros((n_even, D_MODEL, LANES - 2 * GATE_RANK), F32)], axis=2).astype(BF16)
    w_odd = w_in_odd.astype(BF16)
    w_out_bf = w_out.astype(BF16)

    def even_weights(l):
        j = l // 2
        w_vt = w_in_even[j, :, 512:1024].T.astype(BF16)
        gk_f = jnp.pad(w_gk[j, 0], ((0, LANES - GATE_RANK), (0, 0))).astype(BF16)
        gk_b = jnp.pad(w_gk[j, 1], ((GATE_RANK, LANES - 2 * GATE_RANK), (0, 0))).astype(BF16)
        return dict(layer=l, pair=j, norm_g=norm_g[l].reshape(1, D_MODEL), w_in=w_even, w_vt=w_vt,
                    w_gk_f=gk_f, w_gk_b=gk_b,
                    b_gk_f=b_gk[j, 0].reshape(1, 256), b_gk_b=b_gk[j, 1].reshape(1, 256),
                    sgu_norm_g=sgu_norm_g[j].reshape(1, BRANCH_W), w_s=w_s[j].astype(BF16),
                    b_s=jnp.broadcast_to(b_s[j][:, :, None], (H_B, CHUNK_B, LANES)),
                    gla_norm_g=gla_norm_g[j].reshape(1, DV_A), w_out=w_out_bf,
                    tri_lo=tri_lo, tri_up=tri_up)

    def odd_weights(l):
        j = l // 2
        return dict(layer=l, pair=j, norm_g=norm_g[l].reshape(1, D_MODEL), w_in=w_odd,
                    w_pool=jax.scipy.linalg.block_diag(*w_pool[j]).astype(BF16),
                    pool_scale=pool_scale[j].reshape(1, BRANCH_W), w_conv=w_conv[j], w_out=w_out_bf)

    weights = [even_weights(l) if l % 2 == 0 else odd_weights(l) for l in range(DEPTH)]

    def run_stream(x, batch, seq, mod_row_of, inits, latent):
        x = x.reshape(batch * seq, D_MODEL)
        finals = []
        for l in range(DEPTH):
            mod_row = functools.partial(mod_row_of, l)
            if l % 2 == 0:
                x, s_f, s_b = _even_layer(x, mod, mod_row, batch, seq, weights[l], inits)
                finals += [s_f, s_b]
            else:
                last = l == DEPTH - 1
                x = _odd_layer(x, mod, mod_row, batch, seq, weights[l], colmajor_out=latent,
                               final_norm=last, final_norm_g=fng)
        return x.reshape(batch, seq, D_MODEL), finals

    y_prompt, finals = run_stream(x_prompt, bp, tp, lambda l, b: l * MOD_ROWS, None, False)
    new_state = jnp.stack(finals, axis=1).reshape(bp, DEPTH // 2, 2, H_A, DK_A, DV_A)

    y_sample, _ = run_stream(x_sample, bs, ts, lambda l, b: l * MOD_ROWS + 1 + b, state_gla, True)
    return (y_prompt, y_sample, new_state)
```

```python
import functools

import numpy as np
import jax
import jax.numpy as jnp
from jax import lax
from jax.experimental import pallas as pl
from jax.experimental.pallas import tpu as pltpu

F32 = jnp.float32
BF16 = jnp.bfloat16

D_MODEL = 1024
DEPTH = 4
GRID_W = 64
BRANCH_W = 512
H_A = 4
DK_A = 64
DV_A = 128
GATE_RANK = 16
GATE_NORMALIZER = 16.0
CHUNK_A = 64
H_B = 4
DH_B = 128
CHUNK_B = 128
POOL_WINDOWS = (2, 4, 8, 16)
G_C = 4
DG_C = 128
EPS = 1e-6

LANES = 128
GLA_BLOCK = 256
EVEN_TOKEN_BLOCK = 1024
ODD_TOKEN_BLOCK = 1024
ODD_SLAB = 256
HALO = 8
MOD_ROWS = 16
VMEM_LIMIT = 56 * 1024 * 1024
N_PAIR = H_A // 2
PAIR_V = 2 * DV_A


def _dot(a, b):
    return jnp.dot(a, b, preferred_element_type=F32)


def _dot_nt(a, b):
    return lax.dot_general(a, b, (((1,), (1,)), ((), ())), preferred_element_type=F32)


def _rms(x, g):
    ms = jnp.mean(x * x, axis=-1, keepdims=True)
    return x * lax.rsqrt(ms + EPS) * g


def _silu(x):
    return x * (1.0 / (1.0 + jnp.exp(-x)))


def _log_sigmoid(x):
    return jnp.minimum(x, 0.0) - jnp.log1p(jnp.exp(-jnp.abs(x)))


def _split2_bf16(x):
    x1 = x.astype(BF16)
    x2 = (x - x1.astype(F32)).astype(BF16)
    return jnp.concatenate([x1, x2], axis=1)


def _cumsum_blocks(tri, g):
    w = g.shape[1]
    bc = _dot(tri, _split2_bf16(g))
    return bc[:, 0:w] + bc[:, w:2 * w]


def _log_decay(lr_bf16, wgk_ref, bgk_ref):
    gl = _dot(lr_bf16, wgk_ref[...]) + bgk_ref[...]
    return _log_sigmoid(gl) * (1.0 / GATE_NORMALIZER)


def _gla_block(q, k, b, vt_ref, vt_cols, st_scr, reverse, emit, fill=(lambda: None, lambda: None)):
    tb = q.shape[0]
    nc = tb // CHUNK_A
    shift = CHUNK_A.bit_length() - 1

    def chunk_rows(idx):
        return jnp.concatenate(
            [jnp.broadcast_to(b[c * CHUNK_A + idx:c * CHUNK_A + idx + 1], (CHUNK_A, b.shape[1]))
             for c in range(nc)], axis=0)

    end_idx = 0 if reverse else CHUNK_A - 1
    b_end = chunk_rows(end_idx)
    b_mid = chunk_rows(CHUNK_A // 2)
    qs = q * jnp.exp(b)
    qa = q * jnp.exp(b - b_mid)
    ka = (k * jnp.exp(b_mid - b)).astype(BF16)
    kh = k * jnp.exp(b_end - b)
    dec = [jnp.exp(b[c * CHUNK_A + end_idx:c * CHUNK_A + end_idx + 1]) for c in range(nc)]

    zero_blk = jnp.zeros((CHUNK_A, LANES), BF16)

    def route(a):
        return jnp.concatenate(
            [jnp.concatenate([a[c * CHUNK_A:(c + 1) * CHUNK_A] if j == c else zero_blk for j in range(nc)], axis=1)
             for c in range(nc)], axis=0)

    r = lax.broadcasted_iota(jnp.int32, (tb, tb), 0)
    cc = lax.broadcasted_iota(jnp.int32, (tb, tb), 1)
    same_chunk = jnp.right_shift(r, shift) == jnp.right_shift(cc, shift)
    score_mask = jnp.logical_and(same_chunk, (cc >= r) if reverse else (cc <= r))
    lane = lax.broadcasted_iota(jnp.int32, (tb, LANES), 1)
    lane_h0 = lane < DK_A
    srow = lax.broadcasted_iota(jnp.int32, (PAIR_V, LANES), 0)
    slane = lax.broadcasted_iota(jnp.int32, (PAIR_V, LANES), 1)
    own_lanes = (srow < DV_A) == (slane < DK_A)
    zero_vt = jnp.zeros((DV_A, tb), BF16)
    order = range(nc - 1, -1, -1) if reverse else range(nc)

    pairs = [slice(p * LANES, (p + 1) * LANES) for p in range(N_PAIR)]
    vts = [vt_ref[p * PAIR_V:(p + 1) * PAIR_V, vt_cols] for p in range(N_PAIR)]

    att = []
    for sl in pairs:
        qa_p = qa[:, sl]
        lhs_a = jnp.concatenate([jnp.where(lane_h0, qa_p, 0.0), jnp.where(lane_h0, 0.0, qa_p)],
                                axis=0).astype(BF16)
        att2 = _dot_nt(lhs_a, ka[:, sl])
        att.append((jnp.where(score_mask, att2[0:tb], 0.0).astype(BF16),
                    jnp.where(score_mask, att2[tb:2 * tb], 0.0).astype(BF16)))
    fill[0]()

    entering = []
    for p, sl in enumerate(pairs):
        kh_routed = route(kh[:, sl].astype(BF16))
        kv = _dot(vts[p], kh_routed)
        st = st_scr[p]
        ent = [None] * nc
        for c in order:
            ent[c] = st.astype(BF16)
            st = st * dec[c][:, sl] + jnp.where(own_lanes, kv[:, c * LANES:(c + 1) * LANES], 0.0)
        st_scr[p] = st
        entering.append(ent)
    fill[1]()

    for p, sl in enumerate(pairs):
        qs_routed = route(qs[:, sl].astype(BF16))
        vt_p = vts[p]
        vt_diag = jnp.concatenate(
            [jnp.concatenate([vt_p[0:DV_A], zero_vt], axis=1),
             jnp.concatenate([zero_vt, vt_p[DV_A:PAIR_V]], axis=1)], axis=0)
        w_nt = jnp.concatenate([vt_diag] + entering[p], axis=1)
        lhs_o = jnp.concatenate([att[p][0], att[p][1], qs_routed], axis=1)
        emit(p, _dot_nt(lhs_o, w_nt))


def _load_state(st_scr, st0_ref, slot):
    for h in range(H_A):
        hp = h % 2
        zero = jnp.zeros((DK_A, DV_A), F32)
        s = st0_ref[slot, h]
        padded = jnp.concatenate([s, zero] if hp == 0 else [zero, s], axis=0)
        st_scr[h // 2, hp * DV_A:(hp + 1) * DV_A, :] = padded.T


def _write_final_state(st_scr, stout_ref, slot):
    for h in range(H_A):
        hp = h % 2
        st_t = st_scr[h // 2, hp * DV_A:(hp + 1) * DV_A, :].T
        stout_ref[slot, h] = st_t[hp * DK_A:(hp + 1) * DK_A, :]


def _even_fwd_kernel(x_ref, mod_ref, ng_ref, wqk_ref, wvs_ref, wvt_ref, wlr_ref, wgk_ref, bgk_ref, sng_ref, ws_ref,
                     bs_ref, tri_ref, st0_ref,
                     h_ref, qk_ref, vt_ref, lr_ref, of_ref, sp_ref, stout_ref, st_scr, vs_scr, *, carry, zero_state):
    t = pl.program_id(1)
    n_t = pl.num_programs(1)
    tb = x_ref.shape[0]
    n_sub = tb // GLA_BLOCK

    if carry:
        @pl.when(t == 0)
        def _():
            if zero_state:
                st_scr[...] = jnp.zeros(st_scr.shape, F32)
            else:
                _load_state(st_scr, st0_ref, 0)

    x = x_ref[...]
    shift = mod_ref[:, 0:D_MODEL]
    scale = mod_ref[:, D_MODEL:2 * D_MODEL]
    h = (_rms(x, ng_ref[...]) * (1.0 + scale) + shift).astype(BF16)
    h_ref[...] = h
    lr = _dot(h, wlr_ref[...])
    lr_ref[...] = lr
    g = _log_decay(lr.astype(BF16), wgk_ref, bgk_ref)
    qk = _dot(h, wqk_ref[...])
    q = qk[:, 0:256] * (DK_A ** -0.5)
    k = qk[:, 256:512]
    qk_ref[:, 0:256] = q
    qk_ref[:, 256:512] = k
    tri = tri_ref[...]
    b = [_cumsum_blocks(tri, g[s * GLA_BLOCK:(s + 1) * GLA_BLOCK]) for s in range(n_sub)]
    vt_ref[...] = _dot_nt(wvt_ref[...], h).astype(BF16)

    def sgu_input():
        vs = _dot(h, wvs_ref[...])
        vs_scr[...] = _rms(vs, sng_ref[...]).astype(BF16)

    def sgu_mix():
        for n in range(tb // CHUNK_B):
            rs = slice(n * CHUNK_B, (n + 1) * CHUNK_B)
            for hb in range(H_B):
                cs = slice(hb * DH_B, (hb + 1) * DH_B)
                sp_ref[rs, cs] = (_dot(ws_ref[hb], vs_scr[rs, cs]) + bs_ref[hb]).astype(BF16)

    for s in range(n_sub):
        rows = slice(s * GLA_BLOCK, (s + 1) * GLA_BLOCK)
        if not carry:
            if zero_state:
                st_scr[...] = jnp.zeros(st_scr.shape, F32)
            else:
                _load_state(st_scr, st0_ref, s)

        def emit(p, o, rows=rows):
            of_ref[rows, p * PAIR_V:(p + 1) * PAIR_V] = o

        fill = (sgu_input, sgu_mix) if s == 0 else (lambda: None, lambda: None)
        _gla_block(q[rows], k[rows], b[s], vt_ref, rows, st_scr, False, emit, fill)
        if not carry:
            _write_final_state(st_scr, stout_ref, s)

    if carry:
        @pl.when(t == n_t - 1)
        def _():
            _write_final_state(st_scr, stout_ref, 0)


def _even_bwd_kernel(x_ref, mod_ref, h_ref, qk_ref, vt_ref, lr_ref, of_ref, sp_ref, wga_ref, wu_ref, wgb_ref,
                     wgk_ref, bgk_ref,
                     gng_ref, wout_ref, tri_ref, st0_ref,
                     xo_ref, stout_ref, st_scr, mix_scr, outs_scr, *, carry, zero_state):
    t = pl.program_id(1)
    n_t = pl.num_programs(1)
    tb = x_ref.shape[0]
    n_sub = tb // GLA_BLOCK

    if carry:
        @pl.when(t == 0)
        def _():
            if zero_state:
                st_scr[...] = jnp.zeros(st_scr.shape, F32)
            else:
                _load_state(st_scr, st0_ref, 0)

    h = h_ref[...]
    g = _log_decay(lr_ref[...].astype(BF16), wgk_ref, bgk_ref)
    ga = _dot(h, wga_ref[...])
    tri = tri_ref[...]
    b = [_cumsum_blocks(tri, g[s * GLA_BLOCK:(s + 1) * GLA_BLOCK]) for s in range(n_sub)]

    def sgu_gate():
        u = _dot(h, wu_ref[...])
        gate_b = _dot(h, wgb_ref[...])
        mix_scr[:, BRANCH_W:2 * BRANCH_W] = (u * sp_ref[...].astype(F32) * _silu(gate_b)).astype(BF16)

    def sgu_out():
        outs_scr[...] = _dot(mix_scr[:, BRANCH_W:2 * BRANCH_W], wout_ref[BRANCH_W:2 * BRANCH_W, :])

    gng = gng_ref[...]
    order = range(n_sub - 1, -1, -1) if carry else range(n_sub)
    for i, s in enumerate(order):
        rows = slice(s * GLA_BLOCK, (s + 1) * GLA_BLOCK)
        if not carry:
            if zero_state:
                st_scr[...] = jnp.zeros(st_scr.shape, F32)
            else:
                _load_state(st_scr, st0_ref, s)

        def emit(p, o_b, rows=rows):
            for hp in range(2):
                cs = slice(p * PAIR_V + hp * DV_A, p * PAIR_V + (hp + 1) * DV_A)
                o = of_ref[rows, cs] + o_b[:, hp * DV_A:(hp + 1) * DV_A]
                mix_scr[rows, cs] = (_rms(o, gng) * _silu(ga[rows, cs])).astype(BF16)

        fill = (sgu_gate, sgu_out) if i == 0 else (lambda: None, lambda: None)
        _gla_block(qk_ref[rows, 0:256], qk_ref[rows, 256:512], b[s], vt_ref, rows, st_scr, True, emit, fill)
        if not carry:
            _write_final_state(st_scr, stout_ref, s)

    out = _dot(mix_scr[:, 0:BRANCH_W], wout_ref[0:BRANCH_W, :]) + outs_scr[...]
    gate = mod_ref[:, 2 * D_MODEL:3 * D_MODEL]
    xo_ref[...] = x_ref[...] + gate * out

    if carry:
        @pl.when(t == n_t - 1)
        def _():
            _write_final_state(st_scr, stout_ref, 0)


def _window_sum(a, w):
    n = a.shape[0]
    ahead = lambda v, k: pltpu.roll(v, n - k, 0)
    behind = lambda v, k: pltpu.roll(v, k, 0)
    half = w // 2
    p, span = a, 1
    while span < half:
        p = p + ahead(p, span)
        span *= 2
    return (behind(p, half) + p)[HALO:n - HALO]


def _odd_kernel(xp_ref, x_ref, xn_ref, mod_ref, ng_ref, win_ref, wpool_ref, pscale_ref, wconv_ref,
                wout_ref, fng_ref, xo_ref, mix_scr, *, seq_len, n_sub, isolated, colmajor_out, final_norm):
    t = pl.program_id(1)
    n_t = pl.num_programs(1)
    tb = x_ref.shape[0]
    sb = tb // n_sub
    ne = sb + 2 * HALO

    shift = mod_ref[:, 0:D_MODEL]
    scale = mod_ref[:, D_MODEL:2 * D_MODEL]
    gate = mod_ref[:, 2 * D_MODEL:3 * D_MODEL]
    ng = ng_ref[...]
    if isolated:
        xe = x_ref[...]
        bounds = [s * sb for s in range(n_sub + 1)]
    else:
        xe = jnp.concatenate([xp_ref[...], x_ref[...], xn_ref[...]], axis=0)
        bounds = [0] + [ne + s * sb for s in range(n_sub)]

    zp = []
    for s in range(n_sub):
        h = (_rms(xe[bounds[s]:bounds[s + 1]], ng) * (1.0 + scale) + shift).astype(BF16)
        zp.append(_dot(h, win_ref[...]))

    mid = slice(HALO, HALO + sb)
    row_in_slab = lax.broadcasted_iota(jnp.int32, (sb, 1), 0)
    for s in range(n_sub):
        r0 = s * sb
        if isolated:
            zm = zp[s]
            edge = jnp.zeros((HALO, BRANCH_W), F32)
            xc = jnp.concatenate([edge, zm[:, 0:512], edge], axis=0)
            u = jnp.concatenate([edge, zm[:, 2048:2560] * zm[:, 1024:1536], edge], axis=0)
            pos = row_in_slab
            x_mid = xe[r0:r0 + sb]
        else:
            z = zp[0] if s == 0 else jnp.concatenate([zp[s - 1][-2 * HALO:], zp[s]], axis=0)
            rowe = r0 + lax.broadcasted_iota(jnp.int32, (ne, 1), 0)
            valid = jnp.logical_and(jnp.logical_or(rowe >= HALO, t > 0),
                                    jnp.logical_or(rowe < tb + HALO, t < n_t - 1))
            xc = jnp.where(valid, z[:, 0:512], 0.0)
            u = jnp.where(valid, z[:, 2048:2560] * z[:, 1024:1536], 0.0)
            zm = z[mid]
            pos = t * tb + r0 + row_in_slab
            x_mid = xe[HALO + r0:HALO + r0 + sb]

        pooled = []
        for i, w in enumerate(POOL_WINDOWS):
            cs = slice(i * DG_C, (i + 1) * DG_C)
            lo = jnp.clip(pos - w // 2, 0, seq_len)
            hi = jnp.clip(pos + (w - w // 2), 0, seq_len)
            inv_cnt = 1.0 / (hi - lo).astype(F32)
            pooled.append(_window_sum(xc[:, cs], w) * inv_cnt - xc[mid, cs])
        pc = _dot(jnp.concatenate(pooled, axis=1).astype(BF16), wpool_ref[...])
        mix_scr[r0:r0 + sb, 0:BRANCH_W] = (pc * pscale_ref[...] * _silu(zm[:, 512:1024])).astype(BF16)

        y = (pltpu.roll(u, 1, 0) * wconv_ref[0:1, :] + u * wconv_ref[1:2, :]
             + pltpu.roll(u, ne - 1, 0) * wconv_ref[2:3, :])
        mix_scr[r0:r0 + sb, BRANCH_W:2 * BRANCH_W] = (zm[:, 1536:2048] * y[mid]
                                                      * _silu(zm[:, 2560:3072])).astype(BF16)

        out = _dot(mix_scr[r0:r0 + sb, :], wout_ref[...])
        xn = x_mid + gate * out
        if final_norm:
            xn = _rms(xn, fng_ref[...])
        if colmajor_out:
            for j in range(sb // GRID_W):
                xo_ref[:, r0 // GRID_W + j, :] = xn[j * GRID_W:(j + 1) * GRID_W]
        else:
            xo_ref[r0:r0 + sb, :] = xn


def _ada_kernel(c_ref, w_ref, b_ref, o_ref):
    o_ref[...] = _dot(_silu(c_ref[...]).astype(BF16), w_ref[...].astype(BF16)) + b_ref[...]


def _params(n_axes=2):
    return pltpu.CompilerParams(dimension_semantics=("arbitrary",) * n_axes,
                                vmem_limit_bytes=VMEM_LIMIT)


def _ada_mod(conds, w_ada, b_ada):
    n_col = 2
    cw = 3 * D_MODEL // n_col
    return pl.pallas_call(
        _ada_kernel,
        grid=(DEPTH, n_col),
        in_specs=[pl.BlockSpec((MOD_ROWS, D_MODEL), lambda l, j: (0, 0)),
                  pl.BlockSpec((None, D_MODEL, cw), lambda l, j: (l, 0, j)),
                  pl.BlockSpec((None, 1, cw), lambda l, j: (l, 0, j))],
        out_specs=pl.BlockSpec((None, MOD_ROWS, cw), lambda l, j: (l, 0, j)),
        out_shape=jax.ShapeDtypeStruct((DEPTH, MOD_ROWS, 3 * D_MODEL), F32),
        compiler_params=_params(2),
        name="ada_mod",
    )(conds, w_ada, b_ada.reshape(DEPTH, 1, 3 * D_MODEL))


def _tri_blocks(tb, upper):
    r = np.arange(tb)
    same = (r[:, None] // CHUNK_A) == (r[None, :] // CHUNK_A)
    tri = (r[None, :] >= r[:, None]) if upper else (r[None, :] <= r[:, None])
    return jnp.asarray((same & tri).astype(np.float32), dtype=BF16)


def _const_spec(shape):
    nd = len(shape)
    return pl.BlockSpec(shape, lambda b, t: (0,) * nd)


def _even_layer(x, mod, mod_row, batch, seq, lw, init_states):
    tb = EVEN_TOKEN_BLOCK
    carry = seq >= tb
    spp = 1 if carry else tb // seq
    assert seq % tb == 0 if carry else (seq == GLA_BLOCK and batch % spp == 0)
    n_t = seq * spp // tb
    n_grp = batch // spp
    n_tok = batch * seq
    n_blk = n_grp * n_t
    tok = lambda width: pl.BlockSpec((tb, width), lambda b, t: (b * n_t + t, 0))
    tok_rev = lambda width: pl.BlockSpec((tb, width), lambda b, t: (b * n_t + (n_t - 1 - t), 0))
    vt = pl.BlockSpec((N_PAIR * PAIR_V, tb), lambda b, t: (b * n_t + t, 0))
    vt_rev = pl.BlockSpec((N_PAIR * PAIR_V, tb), lambda b, t: (b * n_t + (n_t - 1 - t), 0))
    mod_spec = pl.BlockSpec((None, 1, 3 * D_MODEL), lambda b, t: (mod_row(b * spp), 0, 0))
    zero_state = init_states is None
    j = lw["pair"]
    if zero_state:
        init_states = jnp.zeros((1, 1), F32)
        st_in = [_const_spec((1, 1))] * 2
    else:
        st_in = [pl.BlockSpec((spp, None, None, H_A, DK_A, DV_A), lambda b, t, d=d: (b, j, d, 0, 0, 0))
                 for d in range(2)]
    w_cols = lambda width, blk: pl.BlockSpec((None, D_MODEL, width), lambda b, t: (j, 0, blk))
    w_out_spec = pl.BlockSpec((None, D_MODEL, D_MODEL), lambda b, t: (lw["layer"], 0, 0))
    st_out = pl.BlockSpec((spp, H_A, DK_A, DV_A), lambda b, t: (b, 0, 0, 0))
    st_shape = jax.ShapeDtypeStruct((batch, H_A, DK_A, DV_A), F32)
    st_scratch = pltpu.VMEM((N_PAIR, PAIR_V, LANES), F32)
    gb = GLA_BLOCK

    h, qk, v_t, lr, of, sp, s_f = pl.pallas_call(
        functools.partial(_even_fwd_kernel, carry=carry, zero_state=zero_state),
        grid=(n_grp, n_t),
        in_specs=[tok(D_MODEL), mod_spec, _const_spec((1, D_MODEL)),
                  w_cols(512, 0), w_cols(BRANCH_W, 1), _const_spec((BRANCH_W, D_MODEL)),
                  w_cols(LANES, 0),
                  _const_spec((LANES, 256)), _const_spec((1, 256)), _const_spec((1, BRANCH_W)),
                  _const_spec((H_B, CHUNK_B, CHUNK_B)), _const_spec((H_B, CHUNK_B, LANES)),
                  _const_spec((gb, gb)), st_in[0]],
        out_specs=[tok(D_MODEL), tok(512), vt, tok(LANES), tok(512), tok(512), st_out],
        out_shape=[jax.ShapeDtypeStruct((n_tok, D_MODEL), BF16), jax.ShapeDtypeStruct((n_tok, 512), F32),
                   jax.ShapeDtypeStruct((n_blk * N_PAIR * PAIR_V, tb), BF16),
                   jax.ShapeDtypeStruct((n_tok, LANES), F32),
                   jax.ShapeDtypeStruct((n_tok, 512), F32), jax.ShapeDtypeStruct((n_tok, 512), BF16),
                   st_shape],
        scratch_shapes=[st_scratch, pltpu.VMEM((tb, BRANCH_W), BF16)],
        compiler_params=_params(2),
        name="even_fwd",
    )(x, mod, lw["norm_g"], lw["w_head"], lw["w_tail"], lw["w_vt"], lw["w_lr"], lw["w_gk_f"], lw["b_gk_f"],
      lw["sgu_norm_g"], lw["w_s"], lw["b_s"], lw["tri_lo"], init_states)

    x_new, s_b = pl.pallas_call(
        functools.partial(_even_bwd_kernel, carry=carry, zero_state=zero_state),
        grid=(n_grp, n_t),
        in_specs=[tok_rev(D_MODEL), mod_spec, tok_rev(D_MODEL), tok_rev(512), vt_rev, tok_rev(LANES),
                  tok_rev(512), tok_rev(512), w_cols(BRANCH_W, 2), w_cols(BRANCH_W, 0), w_cols(BRANCH_W, 2),
                  _const_spec((LANES, 256)),
                  _const_spec((1, 256)), _const_spec((1, DV_A)), w_out_spec,
                  _const_spec((gb, gb)), st_in[1]],
        out_specs=[tok_rev(D_MODEL), st_out],
        out_shape=[jax.ShapeDtypeStruct((n_tok, D_MODEL), F32), st_shape],
        scratch_shapes=[st_scratch, pltpu.VMEM((tb, 2 * BRANCH_W), BF16), pltpu.VMEM((tb, D_MODEL), F32)],
        compiler_params=_params(2),
        name="even_bwd",
    )(x, mod, h, qk, v_t, lr, of, sp, lw["w_head"], lw["w_tail"], lw["w_tail"], lw["w_gk_b"], lw["b_gk_b"],
      lw["gla_norm_g"], lw["w_out"], lw["tri_up"], init_states)
    return x_new, s_f, s_b


def _odd_layer(x, mod, mod_row, batch, seq, lw, colmajor_out, final_norm, final_norm_g):
    tb = ODD_TOKEN_BLOCK
    isolated = seq < tb
    spp = tb // seq if isolated else 1
    assert (seq == ODD_SLAB and batch % spp == 0 and not colmajor_out) if isolated else seq % tb == 0
    n_t = seq * spp // tb
    n_grp = batch // spp
    n_tok = batch * seq
    hb = tb // HALO
    n_halo_blocks = n_tok // HALO
    tok = pl.BlockSpec((tb, D_MODEL), lambda b, t: (b * n_t + t, 0))
    prev = pl.BlockSpec((HALO, D_MODEL), lambda b, t: (jnp.maximum((b * n_t + t) * hb - 1, 0), 0))
    nxt = pl.BlockSpec((HALO, D_MODEL),
                       lambda b, t: (jnp.minimum((b * n_t + t + 1) * hb, n_halo_blocks - 1), 0))
    mod_spec = pl.BlockSpec((None, 1, 3 * D_MODEL), lambda b, t: (mod_row(b * spp), 0, 0))
    if colmajor_out:
        rows_per_block = tb // GRID_W
        out_spec = pl.BlockSpec((None, GRID_W, rows_per_block, D_MODEL), lambda b, t: (b, 0, t, 0))
        out_shape = jax.ShapeDtypeStruct((batch, GRID_W, seq // GRID_W, D_MODEL), F32)
    else:
        out_spec = tok
        out_shape = jax.ShapeDtypeStruct((n_tok, D_MODEL), F32)
    kern = functools.partial(_odd_kernel, seq_len=seq, n_sub=tb // ODD_SLAB, isolated=isolated,
                             colmajor_out=colmajor_out, final_norm=final_norm)
    out = pl.pallas_call(
        kern,
        grid=(n_grp, n_t),
        in_specs=[prev, tok, nxt, mod_spec, _const_spec((1, D_MODEL)),
                  pl.BlockSpec((None, D_MODEL, 3072), lambda b, t: (lw["pair"], 0, 0)),
                  _const_spec((BRANCH_W, BRANCH_W)), _const_spec((1, BRANCH_W)), _const_spec((3, BRANCH_W)),
                  pl.BlockSpec((None, D_MODEL, D_MODEL), lambda b, t: (lw["layer"], 0, 0)),
                  _const_spec((1, D_MODEL))],
        out_specs=out_spec,
        out_shape=out_shape,
        scratch_shapes=[pltpu.VMEM((tb, D_MODEL), BF16)],
        compiler_params=_params(2),
        name="odd_layer",
    )(x, x, x, mod, lw["norm_g"], lw["w_in"], lw["w_pool"], lw["pool_scale"], lw["w_conv"], lw["w_out"],
      final_norm_g)
    return out.reshape(n_tok, D_MODEL)


def kernel(x_prompt, x_sample, c, state_gla, c_ctx, w_ada, b_ada, norm_g, w_in_even, w_in_odd, w_out, w_gk,
           b_gk, gla_norm_g, sgu_norm_g, w_s, b_s, w_pool, pool_scale, w_conv, final_norm_g):
    bp, tp, _ = x_prompt.shape
    bs, ts, _ = x_sample.shape
    assert ts == GRID_W * GRID_W
    assert 1 + bs <= MOD_ROWS

    conds = jnp.concatenate([c_ctx[None], c, jnp.zeros((MOD_ROWS - 1 - bs, D_MODEL), F32)], axis=0)
    mod = _ada_mod(conds, w_ada, b_ada).reshape(DEPTH * MOD_ROWS, 1, 3 * D_MODEL)

    tri_lo = _tri_blocks(GLA_BLOCK, upper=False)
    tri_up = _tri_blocks(GLA_BLOCK, upper=True)
    fng = final_norm_g.reshape(1, D_MODEL)

    w_head = w_in_even[:, :, 0:1536].astype(BF16)
    w_tail = w_in_even[:, :, 1568:3104].astype(BF16)
    w_lr = jnp.pad(w_in_even[:, :, 1536:1568], ((0, 0), (0, 0), (0, LANES - 2 * GATE_RANK))).astype(BF16)
    w_odd = w_in_odd.astype(BF16)
    w_out_bf = w_out.astype(BF16)

    def even_weights(l):
        j = l // 2
        w_vt = w_in_even[j, :, 512:1024].T.astype(BF16)
        gk_f = jnp.pad(w_gk[j, 0], ((0, LANES - GATE_RANK), (0, 0))).astype(BF16)
        gk_b = jnp.pad(w_gk[j, 1], ((GATE_RANK, LANES - 2 * GATE_RANK), (0, 0))).astype(BF16)
        return dict(layer=l, pair=j, norm_g=norm_g[l].reshape(1, D_MODEL), w_head=w_head, w_tail=w_tail,
                    w_lr=w_lr, w_vt=w_vt,
                    w_gk_f=gk_f, w_gk_b=gk_b,
                    b_gk_f=b_gk[j, 0].reshape(1, 256), b_gk_b=b_gk[j, 1].reshape(1, 256),
                    sgu_norm_g=sgu_norm_g[j].reshape(1, BRANCH_W), w_s=w_s[j].astype(BF16),
                    b_s=jnp.broadcast_to(b_s[j][:, :, None], (H_B, CHUNK_B, LANES)),
                    gla_norm_g=gla_norm_g[j].reshape(1, DV_A), w_out=w_out_bf,
                    tri_lo=tri_lo, tri_up=tri_up)

    def odd_weights(l):
        j = l // 2
        return dict(layer=l, pair=j, norm_g=norm_g[l].reshape(1, D_MODEL), w_in=w_odd,
                    w_pool=jax.scipy.linalg.block_diag(*w_pool[j]).astype(BF16),
                    pool_scale=pool_scale[j].reshape(1, BRANCH_W), w_conv=w_conv[j], w_out=w_out_bf)

    weights = [even_weights(l) if l % 2 == 0 else odd_weights(l) for l in range(DEPTH)]

    def run_stream(x, batch, seq, mod_row_of, inits, latent):
        x = x.reshape(batch * seq, D_MODEL)
        finals = []
        for l in range(DEPTH):
            mod_row = functools.partial(mod_row_of, l)
            if l % 2 == 0:
                x, s_f, s_b = _even_layer(x, mod, mod_row, batch, seq, weights[l], inits)
                finals += [s_f, s_b]
            else:
                last = l == DEPTH - 1
                x = _odd_layer(x, mod, mod_row, batch, seq, weights[l], colmajor_out=latent,
                               final_norm=last, final_norm_g=fng)
        return x.reshape(batch, seq, D_MODEL), finals

    y_prompt, finals = run_stream(x_prompt, bp, tp, lambda l, b: l * MOD_ROWS, None, False)
    new_state = jnp.stack(finals, axis=1).reshape(bp, DEPTH // 2, 2, H_A, DK_A, DV_A)

    y_sample, _ = run_stream(x_sample, bs, ts, lambda l, b: l * MOD_ROWS + 1 + b, state_gla, True)
    return (y_prompt, y_sample, new_state)
```

```python
import functools

import numpy as np
import jax
import jax.numpy as jnp
from jax import lax
from jax.experimental import pallas as pl
from jax.experimental.pallas import tpu as pltpu

F32 = jnp.float32
BF16 = jnp.bfloat16

D_MODEL = 1024
DEPTH = 4
GRID_W = 64
BRANCH_W = 512
H_A = 4
DK_A = 64
DV_A = 128
GATE_RANK = 16
GATE_NORMALIZER = 16.0
CHUNK_A = 64
H_B = 4
DH_B = 128
CHUNK_B = 128
POOL_WINDOWS = (2, 4, 8, 16)
G_C = 4
DG_C = 128
EPS = 1e-6

LANES = 128
GLA_BLOCK = 256
EVEN_TOKEN_BLOCK = 1024
ODD_TOKEN_BLOCK = 1024
ODD_SLAB = 256
HALO = 8
MOD_ROWS = 16
VMEM_LIMIT = 56 * 1024 * 1024
N_PAIR = H_A // 2
PAIR_V = 2 * DV_A
DECAY_SAFE = 60.0


def _dot(a, b):
    return jnp.dot(a, b, preferred_element_type=F32)


def _dot_nt(a, b):
    return lax.dot_general(a, b, (((1,), (1,)), ((), ())), preferred_element_type=F32)


def _rms(x, g):
    ms = jnp.mean(x * x, axis=-1, keepdims=True)
    return x * lax.rsqrt(ms + EPS) * g


def _silu(x):
    return x * (1.0 / (1.0 + jnp.exp(-x)))


def _log_sigmoid(x):
    return jnp.minimum(x, 0.0) - jnp.log1p(jnp.exp(-jnp.abs(x)))


def _split2_bf16(x):
    x1 = x.astype(BF16)
    x2 = (x - x1.astype(F32)).astype(BF16)
    return jnp.concatenate([x1, x2], axis=1)


def _cumsum_blocks(tri, g):
    w = g.shape[1]
    bc = _dot(tri, _split2_bf16(g))
    return bc[:, 0:w] + bc[:, w:2 * w]


def _log_decay(lr_bf16, wgk_ref, bgk_ref):
    gl = _dot(lr_bf16, wgk_ref[...]) + bgk_ref[...]
    return _log_sigmoid(gl) * (1.0 / GATE_NORMALIZER)


def _scores_pairwise(qp, kp, bp, reverse):
    tb = qp.shape[0]
    nc = tb // CHUNK_A
    rowi = lax.broadcasted_iota(jnp.int32, (CHUNK_A, CHUNK_A), 0)
    coli = lax.broadcasted_iota(jnp.int32, (CHUNK_A, CHUNK_A), 1)
    lane_h0 = lax.broadcasted_iota(jnp.int32, (CHUNK_A, LANES), 1) < DK_A
    zero_blk = jnp.zeros((CHUNK_A, CHUNK_A), BF16)
    rows = [[], []]
    for c in range(nc):
        cr = slice(c * CHUNK_A, (c + 1) * CHUNK_A)
        qc, kc, bc = qp[cr], kp[cr], bp[cr]

        def body(d, acc):
            shift = jnp.where(d == 0, 0, CHUNK_A - d) if reverse else d
            kb = pltpu.roll(kc, shift, 0)
            bb = pltpu.roll(bc, shift, 0)
            w = qc * kb * jnp.exp(jnp.minimum(bc - bb, 0.0))
            s0 = jnp.sum(jnp.where(lane_h0, w, 0.0), axis=1, keepdims=True)
            s1 = jnp.sum(jnp.where(lane_h0, 0.0, w), axis=1, keepdims=True)
            hit = coli == (rowi + d if reverse else rowi - d)
            return acc[0] + jnp.where(hit, s0, 0.0), acc[1] + jnp.where(hit, s1, 0.0)

        zero = jnp.zeros((CHUNK_A, CHUNK_A), F32)
        acc = lax.fori_loop(0, CHUNK_A, body, (zero, zero))
        for hp in range(2):
            rows[hp].append(jnp.concatenate(
                [acc[hp].astype(BF16) if j == c else zero_blk for j in range(nc)], axis=1))
    return jnp.concatenate(rows[0], axis=0), jnp.concatenate(rows[1], axis=0)


def _gla_block(q, k, b, vt_ref, vt_cols, st_scr, reverse, emit, fill=(lambda: None, lambda: None)):
    tb = q.shape[0]
    nc = tb // CHUNK_A
    shift = CHUNK_A.bit_length() - 1

    def chunk_rows(idx):
        return jnp.concatenate(
            [jnp.broadcast_to(b[c * CHUNK_A + idx:c * CHUNK_A + idx + 1], (CHUNK_A, b.shape[1]))
             for c in range(nc)], axis=0)

    end_idx = 0 if reverse else CHUNK_A - 1
    b_end = chunk_rows(end_idx)
    b_mid = chunk_rows(CHUNK_A // 2)
    qs = q * jnp.exp(b)
    qa = q * jnp.exp(b - b_mid)
    ka = (k * jnp.exp(b_mid - b)).astype(BF16)
    kh = k * jnp.exp(b_end - b)
    dec = [jnp.exp(b[c * CHUNK_A + end_idx:c * CHUNK_A + end_idx + 1]) for c in range(nc)]

    zero_blk = jnp.zeros((CHUNK_A, LANES), BF16)

    def route(a):
        return jnp.concatenate(
            [jnp.concatenate([a[c * CHUNK_A:(c + 1) * CHUNK_A] if j == c else zero_blk for j in range(nc)], axis=1)
             for c in range(nc)], axis=0)

    r = lax.broadcasted_iota(jnp.int32, (tb, tb), 0)
    cc = lax.broadcasted_iota(jnp.int32, (tb, tb), 1)
    same_chunk = jnp.right_shift(r, shift) == jnp.right_shift(cc, shift)
    score_mask = jnp.logical_and(same_chunk, (cc >= r) if reverse else (cc <= r))
    lane = lax.broadcasted_iota(jnp.int32, (tb, LANES), 1)
    lane_h0 = lane < DK_A
    srow = lax.broadcasted_iota(jnp.int32, (PAIR_V, LANES), 0)
    slane = lax.broadcasted_iota(jnp.int32, (PAIR_V, LANES), 1)
    own_lanes = (srow < DV_A) == (slane < DK_A)
    zero_vt = jnp.zeros((DV_A, tb), BF16)
    order = range(nc - 1, -1, -1) if reverse else range(nc)

    pairs = [slice(p * LANES, (p + 1) * LANES) for p in range(N_PAIR)]
    vts = [vt_ref[p * PAIR_V:(p + 1) * PAIR_V, vt_cols] for p in range(N_PAIR)]

    def factored_scores():
        out = []
        for sl in pairs:
            qa_p = qa[:, sl]
            lhs_a = jnp.concatenate([jnp.where(lane_h0, qa_p, 0.0), jnp.where(lane_h0, 0.0, qa_p)],
                                    axis=0).astype(BF16)
            att2 = _dot_nt(lhs_a, ka[:, sl])
            out += [jnp.where(score_mask, att2[0:tb], 0.0).astype(BF16),
                    jnp.where(score_mask, att2[tb:2 * tb], 0.0).astype(BF16)]
        return tuple(out)

    def pairwise_scores():
        out = []
        for sl in pairs:
            out += list(_scores_pairwise(q[:, sl], k[:, sl], b[:, sl], reverse))
        return tuple(out)

    spread = jnp.max(jnp.abs(b - b_mid))
    flat = lax.cond(spread <= DECAY_SAFE, factored_scores, pairwise_scores)
    att = [(flat[2 * p], flat[2 * p + 1]) for p in range(N_PAIR)]
    fill[0]()

    entering = []
    for p, sl in enumerate(pairs):
        kh_routed = route(kh[:, sl].astype(BF16))
        kv = _dot(vts[p], kh_routed)
        st = st_scr[p]
        ent = [None] * nc
        for c in order:
            ent[c] = st.astype(BF16)
            st = st * dec[c][:, sl] + jnp.where(own_lanes, kv[:, c * LANES:(c + 1) * LANES], 0.0)
        st_scr[p] = st
        entering.append(ent)
    fill[1]()

    for p, sl in enumerate(pairs):
        qs_routed = route(qs[:, sl].astype(BF16))
        vt_p = vts[p]
        vt_diag = jnp.concatenate(
            [jnp.concatenate([vt_p[0:DV_A], zero_vt], axis=1),
             jnp.concatenate([zero_vt, vt_p[DV_A:PAIR_V]], axis=1)], axis=0)
        w_nt = jnp.concatenate([vt_diag] + entering[p], axis=1)
        lhs_o = jnp.concatenate([att[p][0], att[p][1], qs_routed], axis=1)
        emit(p, _dot_nt(lhs_o, w_nt))


def _load_state(st_scr, st0_ref, slot):
    for h in range(H_A):
        hp = h % 2
        zero = jnp.zeros((DK_A, DV_A), F32)
        s = st0_ref[slot, h]
        padded = jnp.concatenate([s, zero] if hp == 0 else [zero, s], axis=0)
        st_scr[h // 2, hp * DV_A:(hp + 1) * DV_A, :] = padded.T


def _write_final_state(st_scr, stout_ref, slot):
    for h in range(H_A):
        hp = h % 2
        st_t = st_scr[h // 2, hp * DV_A:(hp + 1) * DV_A, :].T
        stout_ref[slot, h] = st_t[hp * DK_A:(hp + 1) * DK_A, :]


def _even_fwd_kernel(x_ref, mod_ref, ng_ref, wqk_ref, wvs_ref, wvt_ref, wlr_ref, wgk_ref, bgk_ref, sng_ref, ws_ref,
                     bs_ref, tri_ref, st0_ref,
                     h_ref, qk_ref, vt_ref, lr_ref, of_ref, sp_ref, stout_ref, st_scr, vs_scr, *, carry, zero_state):
    t = pl.program_id(1)
    n_t = pl.num_programs(1)
    tb = x_ref.shape[0]
    n_sub = tb // GLA_BLOCK

    if carry:
        @pl.when(t == 0)
        def _():
            if zero_state:
                st_scr[...] = jnp.zeros(st_scr.shape, F32)
            else:
                _load_state(st_scr, st0_ref, 0)

    x = x_ref[...]
    shift = mod_ref[:, 0:D_MODEL]
    scale = mod_ref[:, D_MODEL:2 * D_MODEL]
    h = (_rms(x, ng_ref[...]) * (1.0 + scale) + shift).astype(BF16)
    h_ref[...] = h
    lr = _dot(h, wlr_ref[...])
    lr_ref[...] = lr
    g = _log_decay(lr.astype(BF16), wgk_ref, bgk_ref)
    qk = _dot(h, wqk_ref[...])
    q = qk[:, 0:256] * (DK_A ** -0.5)
    k = qk[:, 256:512]
    qk_ref[:, 0:256] = q
    qk_ref[:, 256:512] = k
    tri = tri_ref[...]
    b = [_cumsum_blocks(tri, g[s * GLA_BLOCK:(s + 1) * GLA_BLOCK]) for s in range(n_sub)]
    vt_ref[...] = _dot_nt(wvt_ref[...], h).astype(BF16)

    def sgu_input():
        vs = _dot(h, wvs_ref[...])
        vs_scr[...] = _rms(vs, sng_ref[...]).astype(BF16)

    def sgu_mix():
        for n in range(tb // CHUNK_B):
            rs = slice(n * CHUNK_B, (n + 1) * CHUNK_B)
            for hb in range(H_B):
                cs = slice(hb * DH_B, (hb + 1) * DH_B)
                sp_ref[rs, cs] = (_dot(ws_ref[hb], vs_scr[rs, cs]) + bs_ref[hb]).astype(BF16)

    for s in range(n_sub):
        rows = slice(s * GLA_BLOCK, (s + 1) * GLA_BLOCK)
        if not carry:
            if zero_state:
                st_scr[...] = jnp.zeros(st_scr.shape, F32)
            else:
                _load_state(st_scr, st0_ref, s)

        def emit(p, o, rows=rows):
            of_ref[rows, p * PAIR_V:(p + 1) * PAIR_V] = o

        fill = (sgu_input, sgu_mix) if s == 0 else (lambda: None, lambda: None)
        _gla_block(q[rows], k[rows], b[s], vt_ref, rows, st_scr, False, emit, fill)
        if not carry:
            _write_final_state(st_scr, stout_ref, s)

    if carry:
        @pl.when(t == n_t - 1)
        def _():
            _write_final_state(st_scr, stout_ref, 0)


def _even_bwd_kernel(x_ref, mod_ref, h_ref, qk_ref, vt_ref, lr_ref, of_ref, sp_ref, wga_ref, wu_ref, wgb_ref,
                     wgk_ref, bgk_ref,
                     gng_ref, wout_ref, tri_ref, st0_ref,
                     xo_ref, stout_ref, st_scr, mix_scr, outs_scr, *, carry, zero_state):
    t = pl.program_id(1)
    n_t = pl.num_programs(1)
    tb = x_ref.shape[0]
    n_sub = tb // GLA_BLOCK

    if carry:
        @pl.when(t == 0)
        def _():
            if zero_state:
                st_scr[...] = jnp.zeros(st_scr.shape, F32)
            else:
                _load_state(st_scr, st0_ref, 0)

    h = h_ref[...]
    g = _log_decay(lr_ref[...].astype(BF16), wgk_ref, bgk_ref)
    ga = _dot(h, wga_ref[...])
    tri = tri_ref[...]
    b = [_cumsum_blocks(tri, g[s * GLA_BLOCK:(s + 1) * GLA_BLOCK]) for s in range(n_sub)]

    def sgu_gate():
        u = _dot(h, wu_ref[...])
        gate_b = _dot(h, wgb_ref[...])
        mix_scr[:, BRANCH_W:2 * BRANCH_W] = (u * sp_ref[...].astype(F32) * _silu(gate_b)).astype(BF16)

    def sgu_out():
        outs_scr[...] = _dot(mix_scr[:, BRANCH_W:2 * BRANCH_W], wout_ref[BRANCH_W:2 * BRANCH_W, :])

    gng = gng_ref[...]
    order = range(n_sub - 1, -1, -1) if carry else range(n_sub)
    for i, s in enumerate(order):
        rows = slice(s * GLA_BLOCK, (s + 1) * GLA_BLOCK)
        if not carry:
            if zero_state:
                st_scr[...] = jnp.zeros(st_scr.shape, F32)
            else:
                _load_state(st_scr, st0_ref, s)

        def emit(p, o_b, rows=rows):
            for hp in range(2):
                cs = slice(p * PAIR_V + hp * DV_A, p * PAIR_V + (hp + 1) * DV_A)
                o = of_ref[rows, cs] + o_b[:, hp * DV_A:(hp + 1) * DV_A]
                mix_scr[rows, cs] = (_rms(o, gng) * _silu(ga[rows, cs])).astype(BF16)

        fill = (sgu_gate, sgu_out) if i == 0 else (lambda: None, lambda: None)
        _gla_block(qk_ref[rows, 0:256], qk_ref[rows, 256:512], b[s], vt_ref, rows, st_scr, True, emit, fill)
        if not carry:
            _write_final_state(st_scr, stout_ref, s)

    out = _dot(mix_scr[:, 0:BRANCH_W], wout_ref[0:BRANCH_W, :]) + outs_scr[...]
    gate = mod_ref[:, 2 * D_MODEL:3 * D_MODEL]
    xo_ref[...] = x_ref[...] + gate * out

    if carry:
        @pl.when(t == n_t - 1)
        def _():
            _write_final_state(st_scr, stout_ref, 0)


def _window_sum(a, w):
    n = a.shape[0]
    ahead = lambda v, k: pltpu.roll(v, n - k, 0)
    behind = lambda v, k: pltpu.roll(v, k, 0)
    half = w // 2
    p, span = a, 1
    while span < half:
        p = p + ahead(p, span)
        span *= 2
    return (behind(p, half) + p)[HALO:n - HALO]


def _odd_kernel(xp_ref, x_ref, xn_ref, mod_ref, ng_ref, win_ref, wpool_ref, pscale_ref, wconv_ref,
                wout_ref, fng_ref, xo_ref, mix_scr, *, seq_len, n_sub, isolated, colmajor_out, final_norm):
    t = pl.program_id(1)
    n_t = pl.num_programs(1)
    tb = x_ref.shape[0]
    sb = tb // n_sub
    ne = sb + 2 * HALO

    shift = mod_ref[:, 0:D_MODEL]
    scale = mod_ref[:, D_MODEL:2 * D_MODEL]
    gate = mod_ref[:, 2 * D_MODEL:3 * D_MODEL]
    ng = ng_ref[...]
    if isolated:
        xe = x_ref[...]
        bounds = [s * sb for s in range(n_sub + 1)]
    else:
        xe = jnp.concatenate([xp_ref[...], x_ref[...], xn_ref[...]], axis=0)
        bounds = [0] + [ne + s * sb for s in range(n_sub)]

    zp = []
    for s in range(n_sub):
        h = (_rms(xe[bounds[s]:bounds[s + 1]], ng) * (1.0 + scale) + shift).astype(BF16)
        zp.append(_dot(h, win_ref[...]))

    mid = slice(HALO, HALO + sb)
    row_in_slab = lax.broadcasted_iota(jnp.int32, (sb, 1), 0)
    for s in range(n_sub):
        r0 = s * sb
        if isolated:
            zm = zp[s]
            edge = jnp.zeros((HALO, BRANCH_W), F32)
            xc = jnp.concatenate([edge, zm[:, 0:512], edge], axis=0)
            u = jnp.concatenate([edge, zm[:, 2048:2560] * zm[:, 1024:1536], edge], axis=0)
            pos = row_in_slab
            x_mid = xe[r0:r0 + sb]
        else:
            z = zp[0] if s == 0 else jnp.concatenate([zp[s - 1][-2 * HALO:], zp[s]], axis=0)
            rowe = r0 + lax.broadcasted_iota(jnp.int32, (ne, 1), 0)
            valid = jnp.logical_and(jnp.logical_or(rowe >= HALO, t > 0),
                                    jnp.logical_or(rowe < tb + HALO, t < n_t - 1))
            xc = jnp.where(valid, z[:, 0:512], 0.0)
            u = jnp.where(valid, z[:, 2048:2560] * z[:, 1024:1536], 0.0)
            zm = z[mid]
            pos = t * tb + r0 + row_in_slab
            x_mid = xe[HALO + r0:HALO + r0 + sb]

        pooled = []
        for i, w in enumerate(POOL_WINDOWS):
            cs = slice(i * DG_C, (i + 1) * DG_C)
            lo = jnp.clip(pos - w // 2, 0, seq_len)
            hi = jnp.clip(pos + (w - w // 2), 0, seq_len)
            inv_cnt = 1.0 / (hi - lo).astype(F32)
            pooled.append(_window_sum(xc[:, cs], w) * inv_cnt - xc[mid, cs])
        pc = _dot(jnp.concatenate(pooled, axis=1).astype(BF16), wpool_ref[...])
        mix_scr[r0:r0 + sb, 0:BRANCH_W] = (pc * pscale_ref[...] * _silu(zm[:, 512:1024])).astype(BF16)

        y = (pltpu.roll(u, 1, 0) * wconv_ref[0:1, :] + u * wconv_ref[1:2, :]
             + pltpu.roll(u, ne - 1, 0) * wconv_ref[2:3, :])
        mix_scr[r0:r0 + sb, BRANCH_W:2 * BRANCH_W] = (zm[:, 1536:2048] * y[mid]
                                                      * _silu(zm[:, 2560:3072])).astype(BF16)

        out = _dot(mix_scr[r0:r0 + sb, :], wout_ref[...])
        xn = x_mid + gate * out
        if final_norm:
            xn = _rms(xn, fng_ref[...])
        if colmajor_out:
            for j in range(sb // GRID_W):
                xo_ref[:, r0 // GRID_W + j, :] = xn[j * GRID_W:(j + 1) * GRID_W]
        else:
            xo_ref[r0:r0 + sb, :] = xn


def _ada_kernel(c_ref, w_ref, b_ref, o_ref):
    o_ref[...] = _dot(_silu(c_ref[...]).astype(BF16), w_ref[...].astype(BF16)) + b_ref[...]


def _params(n_axes=2):
    return pltpu.CompilerParams(dimension_semantics=("arbitrary",) * n_axes,
                                vmem_limit_bytes=VMEM_LIMIT)


def _ada_mod(conds, w_ada, b_ada):
    n_col = 2
    cw = 3 * D_MODEL // n_col
    return pl.pallas_call(
        _ada_kernel,
        grid=(DEPTH, n_col),
        in_specs=[pl.BlockSpec((MOD_ROWS, D_MODEL), lambda l, j: (0, 0)),
                  pl.BlockSpec((None, D_MODEL, cw), lambda l, j: (l, 0, j)),
                  pl.BlockSpec((None, 1, cw), lambda l, j: (l, 0, j))],
        out_specs=pl.BlockSpec((None, MOD_ROWS, cw), lambda l, j: (l, 0, j)),
        out_shape=jax.ShapeDtypeStruct((DEPTH, MOD_ROWS, 3 * D_MODEL), F32),
        compiler_params=_params(2),
        name="ada_mod",
    )(conds, w_ada, b_ada.reshape(DEPTH, 1, 3 * D_MODEL))


def _tri_blocks(tb, upper):
    r = np.arange(tb)
    same = (r[:, None] // CHUNK_A) == (r[None, :] // CHUNK_A)
    tri = (r[None, :] >= r[:, None]) if upper else (r[None, :] <= r[:, None])
    return jnp.asarray((same & tri).astype(np.float32), dtype=BF16)


def _const_spec(shape):
    nd = len(shape)
    return pl.BlockSpec(shape, lambda b, t: (0,) * nd)


def _even_layer(x, mod, mod_row, batch, seq, lw, init_states):
    tb = EVEN_TOKEN_BLOCK
    carry = seq >= tb
    spp = 1 if carry else tb // seq
    assert seq % tb == 0 if carry else (seq == GLA_BLOCK and batch % spp == 0)
    n_t = seq * spp // tb
    n_grp = batch // spp
    n_tok = batch * seq
    n_blk = n_grp * n_t
    tok = lambda width: pl.BlockSpec((tb, width), lambda b, t: (b * n_t + t, 0))
    tok_rev = lambda width: pl.BlockSpec((tb, width), lambda b, t: (b * n_t + (n_t - 1 - t), 0))
    vt = pl.BlockSpec((N_PAIR * PAIR_V, tb), lambda b, t: (b * n_t + t, 0))
    vt_rev = pl.BlockSpec((N_PAIR * PAIR_V, tb), lambda b, t: (b * n_t + (n_t - 1 - t), 0))
    mod_spec = pl.BlockSpec((None, 1, 3 * D_MODEL), lambda b, t: (mod_row(b * spp), 0, 0))
    zero_state = init_states is None
    j = lw["pair"]
    if zero_state:
        init_states = jnp.zeros((1, 1), F32)
        st_in = [_const_spec((1, 1))] * 2
    else:
        st_in = [pl.BlockSpec((spp, None, None, H_A, DK_A, DV_A), lambda b, t, d=d: (b, j, d, 0, 0, 0))
                 for d in range(2)]
    w_cols = lambda width, blk: pl.BlockSpec((None, D_MODEL, width), lambda b, t: (j, 0, blk))
    w_out_spec = pl.BlockSpec((None, D_MODEL, D_MODEL), lambda b, t: (lw["layer"], 0, 0))
    st_out = pl.BlockSpec((spp, H_A, DK_A, DV_A), lambda b, t: (b, 0, 0, 0))
    st_shape = jax.ShapeDtypeStruct((batch, H_A, DK_A, DV_A), F32)
    st_scratch = pltpu.VMEM((N_PAIR, PAIR_V, LANES), F32)
    gb = GLA_BLOCK

    h, qk, v_t, lr, of, sp, s_f = pl.pallas_call(
        functools.partial(_even_fwd_kernel, carry=carry, zero_state=zero_state),
        grid=(n_grp, n_t),
        in_specs=[tok(D_MODEL), mod_spec, _const_spec((1, D_MODEL)),
                  w_cols(512, 0), w_cols(BRANCH_W, 1), _const_spec((BRANCH_W, D_MODEL)),
                  w_cols(LANES, 0),
                  _const_spec((LANES, 256)), _const_spec((1, 256)), _const_spec((1, BRANCH_W)),
                  _const_spec((H_B, CHUNK_B, CHUNK_B)), _const_spec((H_B, CHUNK_B, LANES)),
                  _const_spec((gb, gb)), st_in[0]],
        out_specs=[tok(D_MODEL), tok(512), vt, tok(LANES), tok(512), tok(512), st_out],
        out_shape=[jax.ShapeDtypeStruct((n_tok, D_MODEL), BF16), jax.ShapeDtypeStruct((n_tok, 512), F32),
                   jax.ShapeDtypeStruct((n_blk * N_PAIR * PAIR_V, tb), BF16),
                   jax.ShapeDtypeStruct((n_tok, LANES), F32),
                   jax.ShapeDtypeStruct((n_tok, 512), F32), jax.ShapeDtypeStruct((n_tok, 512), BF16),
                   st_shape],
        scratch_shapes=[st_scratch, pltpu.VMEM((tb, BRANCH_W), BF16)],
        compiler_params=_params(2),
        name="even_fwd",
    )(x, mod, lw["norm_g"], lw["w_head"], lw["w_tail"], lw["w_vt"], lw["w_lr"], lw["w_gk_f"], lw["b_gk_f"],
      lw["sgu_norm_g"], lw["w_s"], lw["b_s"], lw["tri_lo"], init_states)

    x_new, s_b = pl.pallas_call(
        functools.partial(_even_bwd_kernel, carry=carry, zero_state=zero_state),
        grid=(n_grp, n_t),
        in_specs=[tok_rev(D_MODEL), mod_spec, tok_rev(D_MODEL), tok_rev(512), vt_rev, tok_rev(LANES),
                  tok_rev(512), tok_rev(512), w_cols(BRANCH_W, 2), w_cols(BRANCH_W, 0), w_cols(BRANCH_W, 2),
                  _const_spec((LANES, 256)),
                  _const_spec((1, 256)), _const_spec((1, DV_A)), w_out_spec,
                  _const_spec((gb, gb)), st_in[1]],
        out_specs=[tok_rev(D_MODEL), st_out],
        out_shape=[jax.ShapeDtypeStruct((n_tok, D_MODEL), F32), st_shape],
        scratch_shapes=[st_scratch, pltpu.VMEM((tb, 2 * BRANCH_W), BF16), pltpu.VMEM((tb, D_MODEL), F32)],
        compiler_params=_params(2),
        name="even_bwd",
    )(x, mod, h, qk, v_t, lr, of, sp, lw["w_head"], lw["w_tail"], lw["w_tail"], lw["w_gk_b"], lw["b_gk_b"],
      lw["gla_norm_g"], lw["w_out"], lw["tri_up"], init_states)
    return x_new, s_f, s_b


def _odd_layer(x, mod, mod_row, batch, seq, lw, colmajor_out, final_norm, final_norm_g):
    tb = ODD_TOKEN_BLOCK
    isolated = seq < tb
    spp = tb // seq if isolated else 1
    assert (seq == ODD_SLAB and batch % spp == 0 and not colmajor_out) if isolated else seq % tb == 0
    n_t = seq * spp // tb
    n_grp = batch // spp
    n_tok = batch * seq
    hb = tb // HALO
    n_halo_blocks = n_tok // HALO
    tok = pl.BlockSpec((tb, D_MODEL), lambda b, t: (b * n_t + t, 0))
    prev = pl.BlockSpec((HALO, D_MODEL), lambda b, t: (jnp.maximum((b * n_t + t) * hb - 1, 0), 0))
    nxt = pl.BlockSpec((HALO, D_MODEL),
                       lambda b, t: (jnp.minimum((b * n_t + t + 1) * hb, n_halo_blocks - 1), 0))
    mod_spec = pl.BlockSpec((None, 1, 3 * D_MODEL), lambda b, t: (mod_row(b * spp), 0, 0))
    if colmajor_out:
        rows_per_block = tb // GRID_W
        out_spec = pl.BlockSpec((None, GRID_W, rows_per_block, D_MODEL), lambda b, t: (b, 0, t, 0))
        out_shape = jax.ShapeDtypeStruct((batch, GRID_W, seq // GRID_W, D_MODEL), F32)
    else:
        out_spec = tok
        out_shape = jax.ShapeDtypeStruct((n_tok, D_MODEL), F32)
    kern = functools.partial(_odd_kernel, seq_len=seq, n_sub=tb // ODD_SLAB, isolated=isolated,
                             colmajor_out=colmajor_out, final_norm=final_norm)
    out = pl.pallas_call(
        kern,
        grid=(n_grp, n_t),
        in_specs=[prev, tok, nxt, mod_spec, _const_spec((1, D_MODEL)),
                  pl.BlockSpec((None, D_MODEL, 3072), lambda b, t: (lw["pair"], 0, 0)),
                  _const_spec((BRANCH_W, BRANCH_W)), _const_spec((1, BRANCH_W)), _const_spec((3, BRANCH_W)),
                  pl.BlockSpec((None, D_MODEL, D_MODEL), lambda b, t: (lw["layer"], 0, 0)),
                  _const_spec((1, D_MODEL))],
        out_specs=out_spec,
        out_shape=out_shape,
        scratch_shapes=[pltpu.VMEM((tb, D_MODEL), BF16)],
        compiler_params=_params(2),
        name="odd_layer",
    )(x, x, x, mod, lw["norm_g"], lw["w_in"], lw["w_pool"], lw["pool_scale"], lw["w_conv"], lw["w_out"],
      final_norm_g)
    return out.reshape(n_tok, D_MODEL)


def kernel(x_prompt, x_sample, c, state_gla, c_ctx, w_ada, b_ada, norm_g, w_in_even, w_in_odd, w_out, w_gk,
           b_gk, gla_norm_g, sgu_norm_g, w_s, b_s, w_pool, pool_scale, w_conv, final_norm_g):
    bp, tp, _ = x_prompt.shape
    bs, ts, _ = x_sample.shape
    assert ts == GRID_W * GRID_W
    assert 1 + bs <= MOD_ROWS

    conds = jnp.concatenate([c_ctx[None], c, jnp.zeros((MOD_ROWS - 1 - bs, D_MODEL), F32)], axis=0)
    mod = _ada_mod(conds, w_ada, b_ada).reshape(DEPTH * MOD_ROWS, 1, 3 * D_MODEL)

    tri_lo = _tri_blocks(GLA_BLOCK, upper=False)
    tri_up = _tri_blocks(GLA_BLOCK, upper=True)
    fng = final_norm_g.reshape(1, D_MODEL)

    w_head = w_in_even[:, :, 0:1536].astype(BF16)
    w_tail = w_in_even[:, :, 1568:3104].astype(BF16)
    w_lr = jnp.pad(w_in_even[:, :, 1536:1568], ((0, 0), (0, 0), (0, LANES - 2 * GATE_RANK))).astype(BF16)
    w_odd = w_in_odd.astype(BF16)
    w_out_bf = w_out.astype(BF16)

    def even_weights(l):
        j = l // 2
        w_vt = w_in_even[j, :, 512:1024].T.astype(BF16)
        gk_f = jnp.pad(w_gk[j, 0], ((0, LANES - GATE_RANK), (0, 0))).astype(BF16)
        gk_b = jnp.pad(w_gk[j, 1], ((GATE_RANK, LANES - 2 * GATE_RANK), (0, 0))).astype(BF16)
        return dict(layer=l, pair=j, norm_g=norm_g[l].reshape(1, D_MODEL), w_head=w_head, w_tail=w_tail,
                    w_lr=w_lr, w_vt=w_vt,
                    w_gk_f=gk_f, w_gk_b=gk_b,
                    b_gk_f=b_gk[j, 0].reshape(1, 256), b_gk_b=b_gk[j, 1].reshape(1, 256),
                    sgu_norm_g=sgu_norm_g[j].reshape(1, BRANCH_W), w_s=w_s[j].astype(BF16),
                    b_s=jnp.broadcast_to(b_s[j][:, :, None], (H_B, CHUNK_B, LANES)),
                    gla_norm_g=gla_norm_g[j].reshape(1, DV_A), w_out=w_out_bf,
                    tri_lo=tri_lo, tri_up=tri_up)

    def odd_weights(l):
        j = l // 2
        return dict(layer=l, pair=j, norm_g=norm_g[l].reshape(1, D_MODEL), w_in=w_odd,
                    w_pool=jax.scipy.linalg.block_diag(*w_pool[j]).astype(BF16),
                    pool_scale=pool_scale[j].reshape(1, BRANCH_W), w_conv=w_conv[j], w_out=w_out_bf)

    weights = [even_weights(l) if l % 2 == 0 else odd_weights(l) for l in range(DEPTH)]

    def run_stream(x, batch, seq, mod_row_of, inits, latent):
        x = x.reshape(batch * seq, D_MODEL)
        finals = []
        for l in range(DEPTH):
            mod_row = functools.partial(mod_row_of, l)
            if l % 2 == 0:
                x, s_f, s_b = _even_layer(x, mod, mod_row, batch, seq, weights[l], inits)
                finals += [s_f, s_b]
            else:
                last = l == DEPTH - 1
                x = _odd_layer(x, mod, mod_row, batch, seq, weights[l], colmajor_out=latent,
                               final_norm=last, final_norm_g=fng)
        return x.reshape(batch, seq, D_MODEL), finals

    y_prompt, finals = run_stream(x_prompt, bp, tp, lambda l, b: l * MOD_ROWS, None, False)
    new_state = jnp.stack(finals, axis=1).reshape(bp, DEPTH // 2, 2, H_A, DK_A, DV_A)

    y_sample, _ = run_stream(x_sample, bs, ts, lambda l, b: l * MOD_ROWS + 1 + b, state_gla, True)
    return (y_prompt, y_sample, new_state)
```

```python
import functools

import numpy as np
import jax
import jax.numpy as jnp
from jax import lax
from jax.experimental import pallas as pl
from jax.experimental.pallas import tpu as pltpu

F32 = jnp.float32
BF16 = jnp.bfloat16

D_MODEL = 1024
DEPTH = 4
GRID_W = 64
BRANCH_W = 512
H_A = 4
DK_A = 64
DV_A = 128
GATE_RANK = 16
GATE_NORMALIZER = 16.0
CHUNK_A = 64
H_B = 4
DH_B = 128
CHUNK_B = 128
POOL_WINDOWS = (2, 4, 8, 16)
G_C = 4
DG_C = 128
EPS = 1e-6

LANES = 128
GLA_BLOCK = 256
EVEN_TOKEN_BLOCK = 1024
ODD_TOKEN_BLOCK = 1024
ODD_SLAB = 256
HALO = 8
MOD_ROWS = 16
VMEM_LIMIT = 56 * 1024 * 1024
N_PAIR = H_A // 2
PAIR_V = 2 * DV_A
DECAY_SAFE = 60.0


def _dot(a, b):
    return jnp.dot(a, b, preferred_element_type=F32)


def _dot_nt(a, b):
    return lax.dot_general(a, b, (((1,), (1,)), ((), ())), preferred_element_type=F32)


def _rms(x, g):
    ms = jnp.mean(x * x, axis=-1, keepdims=True)
    return x * lax.rsqrt(ms + EPS) * g


def _silu(x):
    return x * (1.0 / (1.0 + jnp.exp(-x)))


def _log_sigmoid(x):
    return jnp.minimum(x, 0.0) - jnp.log1p(jnp.exp(-jnp.abs(x)))


def _split2_bf16(x):
    x1 = x.astype(BF16)
    x2 = (x - x1.astype(F32)).astype(BF16)
    return jnp.concatenate([x1, x2], axis=1)


def _cumsum_blocks(tri, g):
    w = g.shape[1]
    bc = _dot(tri, _split2_bf16(g))
    return bc[:, 0:w] + bc[:, w:2 * w]


def _log_decay(lr_bf16, wgk_ref, bgk_ref):
    gl = _dot(lr_bf16, wgk_ref[...]) + bgk_ref[...]
    return _log_sigmoid(gl) * (1.0 / GATE_NORMALIZER)


def _scores_pairwise(qp, kp, bp, reverse):
    tb = qp.shape[0]
    nc = tb // CHUNK_A
    rowi = lax.broadcasted_iota(jnp.int32, (CHUNK_A, CHUNK_A), 0)
    coli = lax.broadcasted_iota(jnp.int32, (CHUNK_A, CHUNK_A), 1)
    lane_h0 = lax.broadcasted_iota(jnp.int32, (CHUNK_A, LANES), 1) < DK_A
    zero_blk = jnp.zeros((CHUNK_A, CHUNK_A), BF16)
    rows = [[], []]
    for c in range(nc):
        cr = slice(c * CHUNK_A, (c + 1) * CHUNK_A)
        qc, kc, bc = qp[cr], kp[cr], bp[cr]

        def body(d, acc):
            shift = jnp.where(d == 0, 0, CHUNK_A - d) if reverse else d
            kb = pltpu.roll(kc, shift, 0)
            bb = pltpu.roll(bc, shift, 0)
            w = qc * kb * jnp.exp(jnp.minimum(bc - bb, 0.0))
            s0 = jnp.sum(jnp.where(lane_h0, w, 0.0), axis=1, keepdims=True)
            s1 = jnp.sum(jnp.where(lane_h0, 0.0, w), axis=1, keepdims=True)
            hit = coli == (rowi + d if reverse else rowi - d)
            return acc[0] + jnp.where(hit, s0, 0.0), acc[1] + jnp.where(hit, s1, 0.0)

        zero = jnp.zeros((CHUNK_A, CHUNK_A), F32)
        acc = lax.fori_loop(0, CHUNK_A, body, (zero, zero))
        for hp in range(2):
            rows[hp].append(jnp.concatenate(
                [acc[hp].astype(BF16) if j == c else zero_blk for j in range(nc)], axis=1))
    return jnp.concatenate(rows[0], axis=0), jnp.concatenate(rows[1], axis=0)


def _gla_block(q, k, b, vt_ref, vt_cols, st_scr, reverse, pairwise, emit, fill=(lambda: None, lambda: None)):
    tb = q.shape[0]
    nc = tb // CHUNK_A
    shift = CHUNK_A.bit_length() - 1

    def chunk_rows(idx):
        return jnp.concatenate(
            [jnp.broadcast_to(b[c * CHUNK_A + idx:c * CHUNK_A + idx + 1], (CHUNK_A, b.shape[1]))
             for c in range(nc)], axis=0)

    end_idx = 0 if reverse else CHUNK_A - 1
    b_end = chunk_rows(end_idx)
    b_mid = chunk_rows(CHUNK_A // 2)
    qs = q * jnp.exp(b)
    qa = q * jnp.exp(b - b_mid)
    ka = (k * jnp.exp(b_mid - b)).astype(BF16)
    kh = k * jnp.exp(b_end - b)
    dec = [jnp.exp(b[c * CHUNK_A + end_idx:c * CHUNK_A + end_idx + 1]) for c in range(nc)]

    zero_blk = jnp.zeros((CHUNK_A, LANES), BF16)

    def route(a):
        return jnp.concatenate(
            [jnp.concatenate([a[c * CHUNK_A:(c + 1) * CHUNK_A] if j == c else zero_blk for j in range(nc)], axis=1)
             for c in range(nc)], axis=0)

    r = lax.broadcasted_iota(jnp.int32, (tb, tb), 0)
    cc = lax.broadcasted_iota(jnp.int32, (tb, tb), 1)
    same_chunk = jnp.right_shift(r, shift) == jnp.right_shift(cc, shift)
    score_mask = jnp.logical_and(same_chunk, (cc >= r) if reverse else (cc <= r))
    lane = lax.broadcasted_iota(jnp.int32, (tb, LANES), 1)
    lane_h0 = lane < DK_A
    srow = lax.broadcasted_iota(jnp.int32, (PAIR_V, LANES), 0)
    slane = lax.broadcasted_iota(jnp.int32, (PAIR_V, LANES), 1)
    own_lanes = (srow < DV_A) == (slane < DK_A)
    zero_vt = jnp.zeros((DV_A, tb), BF16)
    order = range(nc - 1, -1, -1) if reverse else range(nc)

    pairs = [slice(p * LANES, (p + 1) * LANES) for p in range(N_PAIR)]
    vts = [vt_ref[p * PAIR_V:(p + 1) * PAIR_V, vt_cols] for p in range(N_PAIR)]

    att = []
    for sl in pairs:
        if pairwise:
            att.append(_scores_pairwise(q[:, sl], k[:, sl], b[:, sl], reverse))
            continue
        qa_p = qa[:, sl]
        lhs_a = jnp.concatenate([jnp.where(lane_h0, qa_p, 0.0), jnp.where(lane_h0, 0.0, qa_p)],
                                axis=0).astype(BF16)
        att2 = _dot_nt(lhs_a, ka[:, sl])
        att.append((jnp.where(score_mask, att2[0:tb], 0.0).astype(BF16),
                    jnp.where(score_mask, att2[tb:2 * tb], 0.0).astype(BF16)))
    fill[0]()

    entering = []
    for p, sl in enumerate(pairs):
        kh_routed = route(kh[:, sl].astype(BF16))
        kv = _dot(vts[p], kh_routed)
        st = st_scr[p]
        ent = [None] * nc
        for c in order:
            ent[c] = st.astype(BF16)
            st = st * dec[c][:, sl] + jnp.where(own_lanes, kv[:, c * LANES:(c + 1) * LANES], 0.0)
        st_scr[p] = st
        entering.append(ent)
    fill[1]()

    for p, sl in enumerate(pairs):
        qs_routed = route(qs[:, sl].astype(BF16))
        vt_p = vts[p]
        vt_diag = jnp.concatenate(
            [jnp.concatenate([vt_p[0:DV_A], zero_vt], axis=1),
             jnp.concatenate([zero_vt, vt_p[DV_A:PAIR_V]], axis=1)], axis=0)
        w_nt = jnp.concatenate([vt_diag] + entering[p], axis=1)
        lhs_o = jnp.concatenate([att[p][0], att[p][1], qs_routed], axis=1)
        emit(p, _dot_nt(lhs_o, w_nt))


def _decay_spread(b_blocks):
    worst = None
    for b in b_blocks:
        for c in range(b.shape[0] // CHUNK_A):
            bc = b[c * CHUNK_A:(c + 1) * CHUNK_A]
            m = jnp.max(jnp.abs(bc - bc[CHUNK_A // 2:CHUNK_A // 2 + 1]))
            worst = m if worst is None else jnp.maximum(worst, m)
    return worst


def _scan_by_decay_range(b_blocks, scan):
    safe = _decay_spread(b_blocks) <= DECAY_SAFE

    @pl.when(safe)
    def _():
        scan(False)

    @pl.when(jnp.logical_not(safe))
    def _():
        scan(True)


def _load_state(st_scr, st0_ref, slot):
    for h in range(H_A):
        hp = h % 2
        zero = jnp.zeros((DK_A, DV_A), F32)
        s = st0_ref[slot, h]
        padded = jnp.concatenate([s, zero] if hp == 0 else [zero, s], axis=0)
        st_scr[h // 2, hp * DV_A:(hp + 1) * DV_A, :] = padded.T


def _write_final_state(st_scr, stout_ref, slot):
    for h in range(H_A):
        hp = h % 2
        st_t = st_scr[h // 2, hp * DV_A:(hp + 1) * DV_A, :].T
        stout_ref[slot, h] = st_t[hp * DK_A:(hp + 1) * DK_A, :]


def _even_fwd_kernel(x_ref, mod_ref, ng_ref, wqk_ref, wvs_ref, wvt_ref, wlr_ref, wgk_ref, bgk_ref, sng_ref, ws_ref,
                     bs_ref, tri_ref, st0_ref,
                     h_ref, qk_ref, vt_ref, lr_ref, of_ref, sp_ref, stout_ref, st_scr, vs_scr, *, carry, zero_state):
    t = pl.program_id(1)
    n_t = pl.num_programs(1)
    tb = x_ref.shape[0]
    n_sub = tb // GLA_BLOCK

    if carry:
        @pl.when(t == 0)
        def _():
            if zero_state:
                st_scr[...] = jnp.zeros(st_scr.shape, F32)
            else:
                _load_state(st_scr, st0_ref, 0)

    x = x_ref[...]
    shift = mod_ref[:, 0:D_MODEL]
    scale = mod_ref[:, D_MODEL:2 * D_MODEL]
    h = (_rms(x, ng_ref[...]) * (1.0 + scale) + shift).astype(BF16)
    h_ref[...] = h
    lr = _dot(h, wlr_ref[...])
    lr_ref[...] = lr
    g = _log_decay(lr.astype(BF16), wgk_ref, bgk_ref)
    qk = _dot(h, wqk_ref[...])
    q = qk[:, 0:256] * (DK_A ** -0.5)
    k = qk[:, 256:512]
    qk_ref[:, 0:256] = q
    qk_ref[:, 256:512] = k
    tri = tri_ref[...]
    b = [_cumsum_blocks(tri, g[s * GLA_BLOCK:(s + 1) * GLA_BLOCK]) for s in range(n_sub)]
    vt_ref[...] = _dot_nt(wvt_ref[...], h).astype(BF16)

    def sgu_input():
        vs = _dot(h, wvs_ref[...])
        vs_scr[...] = _rms(vs, sng_ref[...]).astype(BF16)

    def sgu_mix():
        for n in range(tb // CHUNK_B):
            rs = slice(n * CHUNK_B, (n + 1) * CHUNK_B)
            for hb in range(H_B):
                cs = slice(hb * DH_B, (hb + 1) * DH_B)
                sp_ref[rs, cs] = (_dot(ws_ref[hb], vs_scr[rs, cs]) + bs_ref[hb]).astype(BF16)

    def scan(pairwise):
        for s in range(n_sub):
            rows = slice(s * GLA_BLOCK, (s + 1) * GLA_BLOCK)
            if not carry:
                if zero_state:
                    st_scr[...] = jnp.zeros(st_scr.shape, F32)
                else:
                    _load_state(st_scr, st0_ref, s)

            def emit(p, o, rows=rows):
                of_ref[rows, p * PAIR_V:(p + 1) * PAIR_V] = o

            fill = (sgu_input, sgu_mix) if s == 0 else (lambda: None, lambda: None)
            _gla_block(q[rows], k[rows], b[s], vt_ref, rows, st_scr, False, pairwise, emit, fill)
            if not carry:
                _write_final_state(st_scr, stout_ref, s)

    _scan_by_decay_range(b, scan)

    if carry:
        @pl.when(t == n_t - 1)
        def _():
            _write_final_state(st_scr, stout_ref, 0)


def _even_bwd_kernel(x_ref, mod_ref, h_ref, qk_ref, vt_ref, lr_ref, of_ref, sp_ref, wga_ref, wu_ref, wgb_ref,
                     wgk_ref, bgk_ref,
                     gng_ref, wout_ref, tri_ref, st0_ref,
                     xo_ref, stout_ref, st_scr, mix_scr, outs_scr, *, carry, zero_state):
    t = pl.program_id(1)
    n_t = pl.num_programs(1)
    tb = x_ref.shape[0]
    n_sub = tb // GLA_BLOCK

    if carry:
        @pl.when(t == 0)
        def _():
            if zero_state:
                st_scr[...] = jnp.zeros(st_scr.shape, F32)
            else:
                _load_state(st_scr, st0_ref, 0)

    h = h_ref[...]
    g = _log_decay(lr_ref[...].astype(BF16), wgk_ref, bgk_ref)
    ga = _dot(h, wga_ref[...])
    tri = tri_ref[...]
    b = [_cumsum_blocks(tri, g[s * GLA_BLOCK:(s + 1) * GLA_BLOCK]) for s in range(n_sub)]

    def sgu_gate():
        u = _dot(h, wu_ref[...])
        gate_b = _dot(h, wgb_ref[...])
        mix_scr[:, BRANCH_W:2 * BRANCH_W] = (u * sp_ref[...].astype(F32) * _silu(gate_b)).astype(BF16)

    def sgu_out():
        outs_scr[...] = _dot(mix_scr[:, BRANCH_W:2 * BRANCH_W], wout_ref[BRANCH_W:2 * BRANCH_W, :])

    gng = gng_ref[...]
    order = range(n_sub - 1, -1, -1) if carry else range(n_sub)

    def scan(pairwise):
        for i, s in enumerate(order):
            rows = slice(s * GLA_BLOCK, (s + 1) * GLA_BLOCK)
            if not carry:
                if zero_state:
                    st_scr[...] = jnp.zeros(st_scr.shape, F32)
                else:
                    _load_state(st_scr, st0_ref, s)

            def emit(p, o_b, rows=rows):
                for hp in range(2):
                    cs = slice(p * PAIR_V + hp * DV_A, p * PAIR_V + (hp + 1) * DV_A)
                    o = of_ref[rows, cs] + o_b[:, hp * DV_A:(hp + 1) * DV_A]
                    mix_scr[rows, cs] = (_rms(o, gng) * _silu(ga[rows, cs])).astype(BF16)

            fill = (sgu_gate, sgu_out) if i == 0 else (lambda: None, lambda: None)
            _gla_block(qk_ref[rows, 0:256], qk_ref[rows, 256:512], b[s], vt_ref, rows, st_scr, True, pairwise,
                       emit, fill)
            if not carry:
                _write_final_state(st_scr, stout_ref, s)

    _scan_by_decay_range(b, scan)

    out = _dot(mix_scr[:, 0:BRANCH_W], wout_ref[0:BRANCH_W, :]) + outs_scr[...]
    gate = mod_ref[:, 2 * D_MODEL:3 * D_MODEL]
    xo_ref[...] = x_ref[...] + gate * out

    if carry:
        @pl.when(t == n_t - 1)
        def _():
            _write_final_state(st_scr, stout_ref, 0)


def _window_sum(a, w):
    n = a.shape[0]
    ahead = lambda v, k: pltpu.roll(v, n - k, 0)
    behind = lambda v, k: pltpu.roll(v, k, 0)
    half = w // 2
    p, span = a, 1
    while span < half:
        p = p + ahead(p, span)
        span *= 2
    return (behind(p, half) + p)[HALO:n - HALO]


def _odd_kernel(xp_ref, x_ref, xn_ref, mod_ref, ng_ref, win_ref, wpool_ref, pscale_ref, wconv_ref,
                wout_ref, fng_ref, xo_ref, mix_scr, *, seq_len, n_sub, isolated, colmajor_out, final_norm):
    t = pl.program_id(1)
    n_t = pl.num_programs(1)
    tb = x_ref.shape[0]
    sb = tb // n_sub
    ne = sb + 2 * HALO

    shift = mod_ref[:, 0:D_MODEL]
    scale = mod_ref[:, D_MODEL:2 * D_MODEL]
    gate = mod_ref[:, 2 * D_MODEL:3 * D_MODEL]
    ng = ng_ref[...]
    if isolated:
        xe = x_ref[...]
        bounds = [s * sb for s in range(n_sub + 1)]
    else:
        xe = jnp.concatenate([xp_ref[...], x_ref[...], xn_ref[...]], axis=0)
        bounds = [0] + [ne + s * sb for s in range(n_sub)]

    zp = []
    for s in range(n_sub):
        h = (_rms(xe[bounds[s]:bounds[s + 1]], ng) * (1.0 + scale) + shift).astype(BF16)
        zp.append(_dot(h, win_ref[...]))

    mid = slice(HALO, HALO + sb)
    row_in_slab = lax.broadcasted_iota(jnp.int32, (sb, 1), 0)
    for s in range(n_sub):
        r0 = s * sb
        if isolated:
            zm = zp[s]
            edge = jnp.zeros((HALO, BRANCH_W), F32)
            xc = jnp.concatenate([edge, zm[:, 0:512], edge], axis=0)
            u = jnp.concatenate([edge, zm[:, 2048:2560] * zm[:, 1024:1536], edge], axis=0)
            pos = row_in_slab
            x_mid = xe[r0:r0 + sb]
        else:
            z = zp[0] if s == 0 else jnp.concatenate([zp[s - 1][-2 * HALO:], zp[s]], axis=0)
            rowe = r0 + lax.broadcasted_iota(jnp.int32, (ne, 1), 0)
            valid = jnp.logical_and(jnp.logical_or(rowe >= HALO, t > 0),
                                    jnp.logical_or(rowe < tb + HALO, t < n_t - 1))
            xc = jnp.where(valid, z[:, 0:512], 0.0)
            u = jnp.where(valid, z[:, 2048:2560] * z[:, 1024:1536], 0.0)
            zm = z[mid]
            pos = t * tb + r0 + row_in_slab
            x_mid = xe[HALO + r0:HALO + r0 + sb]

        pooled = []
        for i, w in enumerate(POOL_WINDOWS):
            cs = slice(i * DG_C, (i + 1) * DG_C)
            lo = jnp.clip(pos - w // 2, 0, seq_len)
            hi = jnp.clip(pos + (w - w // 2), 0, seq_len)
            inv_cnt = 1.0 / (hi - lo).astype(F32)
            pooled.append(_window_sum(xc[:, cs], w) * inv_cnt - xc[mid, cs])
        pc = _dot(jnp.concatenate(pooled, axis=1).astype(BF16), wpool_ref[...])
        mix_scr[r0:r0 + sb, 0:BRANCH_W] = (pc * pscale_ref[...] * _silu(zm[:, 512:1024])).astype(BF16)

        y = (pltpu.roll(u, 1, 0) * wconv_ref[0:1, :] + u * wconv_ref[1:2, :]
             + pltpu.roll(u, ne - 1, 0) * wconv_ref[2:3, :])
        mix_scr[r0:r0 + sb, BRANCH_W:2 * BRANCH_W] = (zm[:, 1536:2048] * y[mid]
                                                      * _silu(zm[:, 2560:3072])).astype(BF16)

        out = _dot(mix_scr[r0:r0 + sb, :], wout_ref[...])
        xn = x_mid + gate * out
        if final_norm:
            xn = _rms(xn, fng_ref[...])
        if colmajor_out:
            for j in range(sb // GRID_W):
                xo_ref[:, r0 // GRID_W + j, :] = xn[j * GRID_W:(j + 1) * GRID_W]
        else:
            xo_ref[r0:r0 + sb, :] = xn


def _ada_kernel(c_ref, w_ref, b_ref, o_ref):
    o_ref[...] = _dot(_silu(c_ref[...]).astype(BF16), w_ref[...].astype(BF16)) + b_ref[...]


def _params(n_axes=2):
    return pltpu.CompilerParams(dimension_semantics=("arbitrary",) * n_axes,
                                vmem_limit_bytes=VMEM_LIMIT)


def _ada_mod(conds, w_ada, b_ada):
    n_col = 2
    cw = 3 * D_MODEL // n_col
    return pl.pallas_call(
        _ada_kernel,
        grid=(DEPTH, n_col),
        in_specs=[pl.BlockSpec((MOD_ROWS, D_MODEL), lambda l, j: (0, 0)),
                  pl.BlockSpec((None, D_MODEL, cw), lambda l, j: (l, 0, j)),
                  pl.BlockSpec((None, 1, cw), lambda l, j: (l, 0, j))],
        out_specs=pl.BlockSpec((None, MOD_ROWS, cw), lambda l, j: (l, 0, j)),
        out_shape=jax.ShapeDtypeStruct((DEPTH, MOD_ROWS, 3 * D_MODEL), F32),
        compiler_params=_params(2),
        name="ada_mod",
    )(conds, w_ada, b_ada.reshape(DEPTH, 1, 3 * D_MODEL))


def _tri_blocks(tb, upper):
    r = np.arange(tb)
    same = (r[:, None] // CHUNK_A) == (r[None, :] // CHUNK_A)
    tri = (r[None, :] >= r[:, None]) if upper else (r[None, :] <= r[:, None])
    return jnp.asarray((same & tri).astype(np.float32), dtype=BF16)


def _const_spec(shape):
    nd = len(shape)
    return pl.BlockSpec(shape, lambda b, t: (0,) * nd)


def _even_layer(x, mod, mod_row, batch, seq, lw, init_states):
    tb = EVEN_TOKEN_BLOCK
    carry = seq >= tb
    spp = 1 if carry else tb // seq
    assert seq % tb == 0 if carry else (seq == GLA_BLOCK and batch % spp == 0)
    n_t = seq * spp // tb
    n_grp = batch // spp
    n_tok = batch * seq
    n_blk = n_grp * n_t
    tok = lambda width: pl.BlockSpec((tb, width), lambda b, t: (b * n_t + t, 0))
    tok_rev = lambda width: pl.BlockSpec((tb, width), lambda b, t: (b * n_t + (n_t - 1 - t), 0))
    vt = pl.BlockSpec((N_PAIR * PAIR_V, tb), lambda b, t: (b * n_t + t, 0))
    vt_rev = pl.BlockSpec((N_PAIR * PAIR_V, tb), lambda b, t: (b * n_t + (n_t - 1 - t), 0))
    mod_spec = pl.BlockSpec((None, 1, 3 * D_MODEL), lambda b, t: (mod_row(b * spp), 0, 0))
    zero_state = init_states is None
    j = lw["pair"]
    if zero_state:
        init_states = jnp.zeros((1, 1), F32)
        st_in = [_const_spec((1, 1))] * 2
    else:
        st_in = [pl.BlockSpec((spp, None, None, H_A, DK_A, DV_A), lambda b, t, d=d: (b, j, d, 0, 0, 0))
                 for d in range(2)]
    w_cols = lambda width, blk: pl.BlockSpec((None, D_MODEL, width), lambda b, t: (j, 0, blk))
    w_out_spec = pl.BlockSpec((None, D_MODEL, D_MODEL), lambda b, t: (lw["layer"], 0, 0))
    st_out = pl.BlockSpec((spp, H_A, DK_A, DV_A), lambda b, t: (b, 0, 0, 0))
    st_shape = jax.ShapeDtypeStruct((batch, H_A, DK_A, DV_A), F32)
    st_scratch = pltpu.VMEM((N_PAIR, PAIR_V, LANES), F32)
    gb = GLA_BLOCK

    h, qk, v_t, lr, of, sp, s_f = pl.pallas_call(
        functools.partial(_even_fwd_kernel, carry=carry, zero_state=zero_state),
        grid=(n_grp, n_t),
        in_specs=[tok(D_MODEL), mod_spec, _const_spec((1, D_MODEL)),
                  w_cols(512, 0), w_cols(BRANCH_W, 1), _const_spec((BRANCH_W, D_MODEL)),
                  w_cols(LANES, 0),
                  _const_spec((LANES, 256)), _const_spec((1, 256)), _const_spec((1, BRANCH_W)),
                  _const_spec((H_B, CHUNK_B, CHUNK_B)), _const_spec((H_B, CHUNK_B, LANES)),
                  _const_spec((gb, gb)), st_in[0]],
        out_specs=[tok(D_MODEL), tok(512), vt, tok(LANES), tok(512), tok(512), st_out],
        out_shape=[jax.ShapeDtypeStruct((n_tok, D_MODEL), BF16), jax.ShapeDtypeStruct((n_tok, 512), F32),
                   jax.ShapeDtypeStruct((n_blk * N_PAIR * PAIR_V, tb), BF16),
                   jax.ShapeDtypeStruct((n_tok, LANES), F32),
                   jax.ShapeDtypeStruct((n_tok, 512), F32), jax.ShapeDtypeStruct((n_tok, 512), BF16),
                   st_shape],
        scratch_shapes=[st_scratch, pltpu.VMEM((tb, BRANCH_W), BF16)],
        compiler_params=_params(2),
        name="even_fwd",
    )(x, mod, lw["norm_g"], lw["w_head"], lw["w_tail"], lw["w_vt"], lw["w_lr"], lw["w_gk_f"], lw["b_gk_f"],
      lw["sgu_norm_g"], lw["w_s"], lw["b_s"], lw["tri_lo"], init_states)

    x_new, s_b = pl.pallas_call(
        functools.partial(_even_bwd_kernel, carry=carry, zero_state=zero_state),
        grid=(n_grp, n_t),
        in_specs=[tok_rev(D_MODEL), mod_spec, tok_rev(D_MODEL), tok_rev(512), vt_rev, tok_rev(LANES),
                  tok_rev(512), tok_rev(512), w_cols(BRANCH_W, 2), w_cols(BRANCH_W, 0), w_cols(BRANCH_W, 2),
                  _const_spec((LANES, 256)),
                  _const_spec((1, 256)), _const_spec((1, DV_A)), w_out_spec,
                  _const_spec((gb, gb)), st_in[1]],
        out_specs=[tok_rev(D_MODEL), st_out],
        out_shape=[jax.ShapeDtypeStruct((n_tok, D_MODEL), F32), st_shape],
        scratch_shapes=[st_scratch, pltpu.VMEM((tb, 2 * BRANCH_W), BF16), pltpu.VMEM((tb, D_MODEL), F32)],
        compiler_params=_params(2),
        name="even_bwd",
    )(x, mod, h, qk, v_t, lr, of, sp, lw["w_head"], lw["w_tail"], lw["w_tail"], lw["w_gk_b"], lw["b_gk_b"],
      lw["gla_norm_g"], lw["w_out"], lw["tri_up"], init_states)
    return x_new, s_f, s_b


def _odd_layer(x, mod, mod_row, batch, seq, lw, colmajor_out, final_norm, final_norm_g):
    tb = ODD_TOKEN_BLOCK
    isolated = seq < tb
    spp = tb // seq if isolated else 1
    assert (seq == ODD_SLAB and batch % spp == 0 and not colmajor_out) if isolated else seq % tb == 0
    n_t = seq * spp // tb
    n_grp = batch // spp
    n_tok = batch * seq
    hb = tb // HALO
    n_halo_blocks = n_tok // HALO
    tok = pl.BlockSpec((tb, D_MODEL), lambda b, t: (b * n_t + t, 0))
    prev = pl.BlockSpec((HALO, D_MODEL), lambda b, t: (jnp.maximum((b * n_t + t) * hb - 1, 0), 0))
    nxt = pl.BlockSpec((HALO, D_MODEL),
                       lambda b, t: (jnp.minimum((b * n_t + t + 1) * hb, n_halo_blocks - 1), 0))
    mod_spec = pl.BlockSpec((None, 1, 3 * D_MODEL), lambda b, t: (mod_row(b * spp), 0, 0))
    if colmajor_out:
        rows_per_block = tb // GRID_W
        out_spec = pl.BlockSpec((None, GRID_W, rows_per_block, D_MODEL), lambda b, t: (b, 0, t, 0))
        out_shape = jax.ShapeDtypeStruct((batch, GRID_W, seq // GRID_W, D_MODEL), F32)
    else:
        out_spec = tok
        out_shape = jax.ShapeDtypeStruct((n_tok, D_MODEL), F32)
    kern = functools.partial(_odd_kernel, seq_len=seq, n_sub=tb // ODD_SLAB, isolated=isolated,
                             colmajor_out=colmajor_out, final_norm=final_norm)
    out = pl.pallas_call(
        kern,
        grid=(n_grp, n_t),
        in_specs=[prev, tok, nxt, mod_spec, _const_spec((1, D_MODEL)),
                  pl.BlockSpec((None, D_MODEL, 3072), lambda b, t: (lw["pair"], 0, 0)),
                  _const_spec((BRANCH_W, BRANCH_W)), _const_spec((1, BRANCH_W)), _const_spec((3, BRANCH_W)),
                  pl.BlockSpec((None, D_MODEL, D_MODEL), lambda b, t: (lw["layer"], 0, 0)),
                  _const_spec((1, D_MODEL))],
        out_specs=out_spec,
        out_shape=out_shape,
        scratch_shapes=[pltpu.VMEM((tb, D_MODEL), BF16)],
        compiler_params=_params(2),
        name="odd_layer",
    )(x, x, x, mod, lw["norm_g"], lw["w_in"], lw["w_pool"], lw["pool_scale"], lw["w_conv"], lw["w_out"],
      final_norm_g)
    return out.reshape(n_tok, D_MODEL)


def kernel(x_prompt, x_sample, c, state_gla, c_ctx, w_ada, b_ada, norm_g, w_in_even, w_in_odd, w_out, w_gk,
           b_gk, gla_norm_g, sgu_norm_g, w_s, b_s, w_pool, pool_scale, w_conv, final_norm_g):
    bp, tp, _ = x_prompt.shape
    bs, ts, _ = x_sample.shape
    assert ts == GRID_W * GRID_W
    assert 1 + bs <= MOD_ROWS

    conds = jnp.concatenate([c_ctx[None], c, jnp.zeros((MOD_ROWS - 1 - bs, D_MODEL), F32)], axis=0)
    mod = _ada_mod(conds, w_ada, b_ada).reshape(DEPTH * MOD_ROWS, 1, 3 * D_MODEL)

    tri_lo = _tri_blocks(GLA_BLOCK, upper=False)
    tri_up = _tri_blocks(GLA_BLOCK, upper=True)
    fng = final_norm_g.reshape(1, D_MODEL)

    w_head = w_in_even[:, :, 0:1536].astype(BF16)
    w_tail = w_in_even[:, :, 1568:3104].astype(BF16)
    w_lr = jnp.pad(w_in_even[:, :, 1536:1568], ((0, 0), (0, 0), (0, LANES - 2 * GATE_RANK))).astype(BF16)
    w_odd = w_in_odd.astype(BF16)
    w_out_bf = w_out.astype(BF16)

    def even_weights(l):
        j = l // 2
        w_vt = w_in_even[j, :, 512:1024].T.astype(BF16)
        gk_f = jnp.pad(w_gk[j, 0], ((0, LANES - GATE_RANK), (0, 0))).astype(BF16)
        gk_b = jnp.pad(w_gk[j, 1], ((GATE_RANK, LANES - 2 * GATE_RANK), (0, 0))).astype(BF16)
        return dict(layer=l, pair=j, norm_g=norm_g[l].reshape(1, D_MODEL), w_head=w_head, w_tail=w_tail,
                    w_lr=w_lr, w_vt=w_vt,
                    w_gk_f=gk_f, w_gk_b=gk_b,
                    b_gk_f=b_gk[j, 0].reshape(1, 256), b_gk_b=b_gk[j, 1].reshape(1, 256),
                    sgu_norm_g=sgu_norm_g[j].reshape(1, BRANCH_W), w_s=w_s[j].astype(BF16),
                    b_s=jnp.broadcast_to(b_s[j][:, :, None], (H_B, CHUNK_B, LANES)),
                    gla_norm_g=gla_norm_g[j].reshape(1, DV_A), w_out=w_out_bf,
                    tri_lo=tri_lo, tri_up=tri_up)

    def odd_weights(l):
        j = l // 2
        return dict(layer=l, pair=j, norm_g=norm_g[l].reshape(1, D_MODEL), w_in=w_odd,
                    w_pool=jax.scipy.linalg.block_diag(*w_pool[j]).astype(BF16),
                    pool_scale=pool_scale[j].reshape(1, BRANCH_W), w_conv=w_conv[j], w_out=w_out_bf)

    weights = [even_weights(l) if l % 2 == 0 else odd_weights(l) for l in range(DEPTH)]

    def run_stream(x, batch, seq, mod_row_of, inits, latent):
        x = x.reshape(batch * seq, D_MODEL)
        finals = []
        for l in range(DEPTH):
            mod_row = functools.partial(mod_row_of, l)
            if l % 2 == 0:
                x, s_f, s_b = _even_layer(x, mod, mod_row, batch, seq, weights[l], inits)
                finals += [s_f, s_b]
            else:
                last = l == DEPTH - 1
                x = _odd_layer(x, mod, mod_row, batch, seq, weights[l], colmajor_out=latent,
                               final_norm=last, final_norm_g=fng)
        return x.reshape(batch, seq, D_MODEL), finals

    y_prompt, finals = run_stream(x_prompt, bp, tp, lambda l, b: l * MOD_ROWS, None, False)
    new_state = jnp.stack(finals, axis=1).reshape(bp, DEPTH // 2, 2, H_A, DK_A, DV_A)

    y_sample, _ = run_stream(x_sample, bs, ts, lambda l, b: l * MOD_ROWS + 1 + b, state_gla, True)
    return (y_prompt, y_sample, new_state)
```
